```python
import jax, jax.numpy as jnp
from jax import lax
import numpy as np

D_MODEL = 2048
BATCH = 32
SEQ = 256
DEPTH = 2
DEC_BATCH = 4
DEC_SEQ = 4096
PAST_LEN = 256

GRID_W = 64
H_A = 16
Q_LORA = 512
KV_LORA = 256
NOPE = 128
ROPE = 64
V_A = D_MODEL // H_A
ROPE_THETA = 10000.0
H_B = 16
HD_B = D_MODEL // H_B
WIN_R = 8
WIN_C = 16
H_C = 16
DK_C = 128
DV_C = D_MODEL // H_C
CONV_K = 5
CHUNK = 64
D_FF = 4 * D_MODEL
EPS = 1e-6
Q_BLOCK = 128

W_QA = Q_LORA
W_KVA = KV_LORA + ROPE
W_NA = 3 * H_B * HD_B
W_DNQKV = 2 * H_C * DK_C + H_C * DV_C
W_DNZ = H_C * DV_C
W_DNBA = 4 * H_C
W_GATE = 3 * D_MODEL
D_IN = W_QA + W_KVA + W_NA + W_DNQKV + W_DNZ + W_DNBA + W_GATE

kernel_name = "hybrid_mla_natten_deltanet_dit"


def rmsnorm(x, g):
    xf = x.astype(jnp.float32)
    y = xf * lax.rsqrt(jnp.mean(xf * xf, -1, keepdims=True) + EPS)
    return (y * g.astype(jnp.float32)).astype(x.dtype)


def l2norm(x):
    xf = x.astype(jnp.float32)
    return xf * lax.rsqrt(jnp.sum(xf * xf, -1, keepdims=True) + EPS)


def modulate(x, shift, scale):
    return x * (1 + scale) + shift


def ada_mods(cvec, w_ada, b_ada):
    mod = jax.nn.silu(cvec) @ w_ada + b_ada
    return [m[..., None, :] for m in jnp.split(mod, 6, -1)]


def rope_angles(T):
    n_freq = ROPE // 4
    inv = 1.0 / (ROPE_THETA ** (jnp.arange(n_freq, dtype=jnp.float32) / n_freq))
    t = jnp.arange(T)
    row = (t // GRID_W).astype(jnp.float32)
    col = (t % GRID_W).astype(jnp.float32)
    ang = jnp.concatenate([row[:, None] * inv, col[:, None] * inv], -1)
    return jnp.cos(ang), jnp.sin(ang)


def apply_rope(x, cos, sin):
    x1, x2 = jnp.split(x, 2, -1)
    c = cos.astype(x.dtype)
    s = sin.astype(x.dtype)
    return jnp.concatenate([x1 * c - x2 * s, x1 * s + x2 * c], -1)


def blocked_attention(q, k, v):
    B, Tq, H, Dk = q.shape
    nb = Tq // Q_BLOCK
    scale = Dk ** -0.5
    qb = q.reshape(B, nb, Q_BLOCK, H, Dk).transpose(1, 0, 2, 3, 4)

    def one(qblk):
        s = jnp.einsum("bqhd,bkhd->bhqk", qblk, k).astype(jnp.float32) * scale
        p = jax.nn.softmax(s, -1).astype(v.dtype)
        return jnp.einsum("bhqk,bkhd->bqhd", p, v)

    o = lax.map(one, qb)
    return o.transpose(1, 0, 2, 3, 4).reshape(B, Tq, H, v.shape[-1])


def neighbourhood_attention(q, k, v, k_ctx, v_ctx, rpb):
    B, T, H, D = q.shape
    rows = T // GRID_W
    wr = min(WIN_R, rows)
    scale = D ** -0.5
    qg = q.reshape(B, rows, GRID_W, H, D).transpose(1, 0, 2, 3, 4)
    kg = k.reshape(B, rows, GRID_W, H, D)
    vg = v.reshape(B, rows, GRID_W, H, D)
    cols = jnp.arange(GRID_W)
    col_start = jnp.clip(cols - WIN_C // 2, 0, GRID_W - WIN_C)
    col_idx = col_start[:, None] + jnp.arange(WIN_C)
    dc = col_idx - cols[:, None] + (WIN_C - 1)
    n_win = wr * WIN_C

    def one_row(args):
        r, q_row = args
        rs = jnp.clip(r - wr // 2, 0, rows - wr)
        k_band = lax.dynamic_slice_in_dim(kg, rs, wr, axis=1)
        v_band = lax.dynamic_slice_in_dim(vg, rs, wr, axis=1)
        k_win = k_band[:, :, col_idx]
        v_win = v_band[:, :, col_idx]
        dr = rs + jnp.arange(wr) - r + (WIN_R - 1)
        bias = rpb[:, dr][:, :, dc].transpose(0, 2, 1, 3)
        s_win = jnp.einsum("bchd,brcwhd->bhcrw", q_row, k_win) * scale + bias[None]
        s_win = s_win.reshape(B, H, GRID_W, n_win)
        s_ctx = jnp.einsum("bchd,bphd->bhcp", q_row, k_ctx) * scale
        s = jnp.concatenate([s_win, s_ctx], -1).astype(jnp.float32)
        p = jax.nn.softmax(s, -1).astype(v.dtype)
        p_win = p[..., :n_win].reshape(B, H, GRID_W, wr, WIN_C)
        p_ctx = p[..., n_win:]
        return (jnp.einsum("bhcrw,brcwhd->bchd", p_win, v_win)
                + jnp.einsum("bhcp,bphd->bchd", p_ctx, v_ctx))

    o = lax.map(one_row, (jnp.arange(rows), qg))
    return o.transpose(1, 0, 2, 3, 4).reshape(B, T, H, D)


def short_conv(x, w):
    pad = CONV_K // 2
    y = lax.conv_general_dilated(x, w[:, None, :].astype(x.dtype), window_strides=(1,),
                                 padding=[(pad, pad)], dimension_numbers=("NWC", "WIO", "NWC"),
                                 feature_group_count=x.shape[-1])
    return jax.nn.silu(y)


def gated_delta_chunked(q, k, v, g, beta, S0):
    B, T, H, DK = q.shape
    nc = T // CHUNK

    def chunks(x):
        return x.reshape(B, nc, CHUNK, H, -1).transpose(0, 3, 1, 2, 4)

    q = chunks(q) * (DK ** -0.5)
    k = chunks(k)
    v = chunks(v)
    g = chunks(g[..., None])[..., 0]
    beta = chunks(beta[..., None])[..., 0]
    gc = jnp.cumsum(g, -1)
    idx = jnp.arange(CHUNK)
    incl = idx[:, None] >= idx[None, :]
    strict = idx[:, None] > idx[None, :]
    decay = jnp.exp(jnp.where(incl, gc[..., :, None] - gc[..., None, :], -jnp.inf))
    kb = k * beta[..., None]
    vb = v * beta[..., None]
    A = jnp.where(strict, jnp.einsum("bhnid,bhnjd->bhnij", kb, k) * decay, 0.0)
    eye = jnp.eye(CHUNK, dtype=A.dtype)
    Tm = lax.linalg.triangular_solve(eye + A, jnp.broadcast_to(eye, A.shape), left_side=True,
                                     lower=True, unit_diagonal=True)
    w_val = jnp.einsum("bhnij,bhnjv->bhniv", Tm, vb)
    k_cum = jnp.einsum("bhnij,bhnjd->bhnid", Tm, kb * jnp.exp(gc)[..., None])
    attn = jnp.where(incl, jnp.einsum("bhnid,bhnjd->bhnij", q, k) * decay, 0.0)
    g_last = gc[..., -1]
    k_tail = k * jnp.exp(g_last[..., None] - gc)[..., None]
    q_dec = q * jnp.exp(gc)[..., None]
    xs = tuple(jnp.moveaxis(a, 2, 0) for a in (w_val, k_cum, attn, k_tail, q_dec, g_last))

    def step(S, x):
        w_c, kc_c, a_c, kt_c, qd_c, gl_c = x
        v_new = w_c - jnp.einsum("bhck,bhkv->bhcv", kc_c, S)
        o = jnp.einsum("bhck,bhkv->bhcv", qd_c, S) + jnp.einsum("bhij,bhjv->bhiv", a_c, v_new)
        S = S * jnp.exp(gl_c)[..., None, None] + jnp.einsum("bhck,bhcv->bhkv", kt_c, v_new)
        return S, o

    S_fin, o = lax.scan(step, S0, xs)
    o = o.transpose(1, 0, 3, 2, 4).reshape(B, T, H, -1)
    return o, S_fin


def deltanet_branch(qkv, z, ba, p, S_f0, S_b0):
    B, T, _ = qkv.shape
    qkv = short_conv(qkv, p["dn_conv"])
    q, k, v = jnp.split(qkv, [H_C * DK_C, 2 * H_C * DK_C], -1)
    q = l2norm(q.reshape(B, T, H_C, DK_C))
    k = l2norm(k.reshape(B, T, H_C, DK_C))
    v = v.reshape(B, T, H_C, DV_C).astype(jnp.float32)
    b_f, b_b, a_f, a_b = jnp.split(ba.astype(jnp.float32), 4, -1)
    a_log = p["dn_a_log"].astype(jnp.float32)
    dt_b = p["dn_dt_bias"].astype(jnp.float32)
    g_f = -jnp.exp(a_log[0]) * jax.nn.softplus(a_f + dt_b[0])
    g_b = -jnp.exp(a_log[1]) * jax.nn.softplus(a_b + dt_b[1])
    o_f, S_f = gated_delta_chunked(q, k, v, g_f, jax.nn.sigmoid(b_f), S_f0.astype(jnp.float32))
    o_b, S_b = gated_delta_chunked(q[:, ::-1], k[:, ::-1], v[:, ::-1], g_b[:, ::-1],
                                   jax.nn.sigmoid(b_b)[:, ::-1], S_b0.astype(jnp.float32))
    o = o_f + o_b[:, ::-1]
    o = rmsnorm(o, p["dn_norm"]) * jax.nn.silu(z.reshape(B, T, H_C, DV_C).astype(jnp.float32))
    return o.reshape(B, T, H_C * DV_C).astype(qkv.dtype), S_f, S_b


def split_projection(h, w_in):
    offs = np.cumsum([W_QA, W_KVA, W_NA, W_DNQKV, W_DNZ, W_DNBA])
    return jnp.split(h @ w_in, [int(o) for o in offs], -1)


def mla_project(q_a, kv_a, p):
    B, T, _ = q_a.shape
    q = (rmsnorm(q_a, p["mla_q_norm"]) @ p["w_q_b"]).reshape(B, T, H_A, NOPE + ROPE)
    c_kv = rmsnorm(kv_a[..., :KV_LORA], p["mla_kv_norm"])
    k_pe = kv_a[..., KV_LORA:]
    return q, c_kv, k_pe


def mla_keys_values(c_kv, k_pe, w_kv_b):
    B, T, _ = c_kv.shape
    kv = (c_kv @ w_kv_b).reshape(B, T, H_A, NOPE + V_A)
    k = jnp.concatenate([kv[..., :NOPE], jnp.broadcast_to(k_pe[:, :, None, :], (B, T, H_A, ROPE))], -1)
    return k, kv[..., NOPE:]


def merge_branches(gate_logits, o_a, o_b, o_c, w_out):
    g_a, g_b, g_c = jnp.split(jax.nn.sigmoid(gate_logits), 3, -1)
    return (g_a * o_a + g_b * o_b + g_c * o_c) @ w_out


def mixer_context(h, p):
    B, T, _ = h.shape
    qa, kva, na, dnqkv, dnz, dnba, gates = split_projection(h, p["w_in"])
    q, c_kv, k_pe = mla_project(qa, kva, p)
    k, v = mla_keys_values(c_kv, k_pe, p["w_kv_b"])
    o_a = blocked_attention(q, k, v).reshape(B, T, D_MODEL)
    qn, kn, vn = [t.reshape(B, T, H_B, HD_B) for t in jnp.split(na, 3, -1)]
    o_b = blocked_attention(qn, kn, vn).reshape(B, T, D_MODEL)
    zero = jnp.zeros((B, H_C, DK_C, DV_C), jnp.float32)
    o_c, S_f, S_b = deltanet_branch(dnqkv, dnz, dnba, p, zero, zero)
    out = merge_branches(gates, o_a, o_b, o_c, p["w_out"])
    return out, (c_kv, k_pe, kn, vn, S_f.astype(h.dtype), S_b.astype(h.dtype))


def mixer_latent(h, p, ckv_ctx, kpe_ctx, k_ctx, v_ctx, S_f0, S_b0):
    B, T, _ = h.shape
    qa, kva, na, dnqkv, dnz, dnba, gates = split_projection(h, p["w_in"])
    q, c_kv, k_pe = mla_project(qa, kva, p)
    cos, sin = rope_angles(T)
    q = jnp.concatenate([q[..., :NOPE], apply_rope(q[..., NOPE:], cos[:, None, :], sin[:, None, :])], -1)
    k_pe = apply_rope(k_pe, cos, sin)
    k_lat, v_lat = mla_keys_values(c_kv, k_pe, p["w_kv_b"])
    k_c, v_c = mla_keys_values(ckv_ctx, kpe_ctx, p["w_kv_b"])
    o_a = blocked_attention(q, jnp.concatenate([k_lat, k_c], 1),
                            jnp.concatenate([v_lat, v_c], 1)).reshape(B, T, D_MODEL)
    qn, kn, vn = [t.reshape(B, T, H_B, HD_B) for t in jnp.split(na, 3, -1)]
    o_b = neighbourhood_attention(qn, kn, vn, k_ctx, v_ctx, p["na_rpb"]).reshape(B, T, D_MODEL)
    o_c, _, _ = deltanet_branch(dnqkv, dnz, dnba, p, S_f0, S_b0)
    return merge_branches(gates, o_a, o_b, o_c, p["w_out"])


def mlp(h, w1, w2):
    return jnp.square(jax.nn.relu(h @ w1)) @ w2


def setup_inputs(seed: int = 0) -> dict:
    key = jax.random.key(seed)
    ks = iter(jax.random.split(key, 40))
    L = DEPTH

    def nrm(shape, scale):
        return jax.random.normal(next(ks), shape, jnp.float32) * scale

    def gain(shape):
        return 1.0 + nrm(shape, 0.05)

    dt = jnp.exp(jax.random.uniform(next(ks), (L, 2, H_C), jnp.float32, np.log(1e-3), np.log(1e-1)))
    return {
        "x_prompt": nrm((BATCH, SEQ, D_MODEL), 1.0),
        "x_sample": nrm((DEC_BATCH, DEC_SEQ, D_MODEL), 1.0),
        "c": nrm((DEC_BATCH, D_MODEL), 1.0),
        "cache_mla_ckv": nrm((DEC_BATCH, L, PAST_LEN, KV_LORA), 1.0),
        "cache_mla_kpe": nrm((DEC_BATCH, L, PAST_LEN, ROPE), 1.0),
        "cache_na_k": nrm((DEC_BATCH, L, PAST_LEN, H_B, HD_B), 1.0),
        "cache_na_v": nrm((DEC_BATCH, L, PAST_LEN, H_B, HD_B), 1.0),
        "state_dn_fwd": nrm((DEC_BATCH, L, H_C, DK_C, DV_C), 0.1),
        "state_dn_bwd": nrm((DEC_BATCH, L, H_C, DK_C, DV_C), 0.1),
        "c_ctx": nrm((D_MODEL,), 1.0),
        "w_ada": nrm((L, D_MODEL, 6 * D_MODEL), 0.5 * D_MODEL ** -0.5),
        "b_ada": nrm((L, 6 * D_MODEL), 0.01),
        "norm_pre_mix": gain((L, D_MODEL)),
        "norm_post_mix": gain((L, D_MODEL)),
        "norm_pre_mlp": gain((L, D_MODEL)),
        "norm_post_mlp": gain((L, D_MODEL)),
        "w_in": nrm((L, D_MODEL, D_IN), D_MODEL ** -0.5),
        "mla_q_norm": gain((L, Q_LORA)),
        "w_q_b": nrm((L, Q_LORA, H_A * (NOPE + ROPE)), Q_LORA ** -0.5),
        "mla_kv_norm": gain((L, KV_LORA)),
        "w_kv_b": nrm((L, KV_LORA, H_A * (NOPE + V_A)), KV_LORA ** -0.5),
        "na_rpb": nrm((L, H_B, 2 * WIN_R - 1, 2 * WIN_C - 1), 0.1),
        "dn_conv": nrm((L, CONV_K, W_DNQKV), CONV_K ** -0.5),
        "dn_a_log": jnp.log(jax.random.uniform(next(ks), (L, 2, H_C), jnp.float32, 1.0, 16.0)),
        "dn_dt_bias": dt + jnp.log(-jnp.expm1(-dt)),
        "dn_norm": gain((L, DV_C)),
        "w_out": nrm((L, D_MODEL, D_MODEL), D_MODEL ** -0.5),
        "w_mlp_in": nrm((L, D_MODEL, D_FF), D_MODEL ** -0.5),
        "w_mlp_out": nrm((L, D_FF, D_MODEL), D_FF ** -0.5),
    }


def reference(x_prompt, x_sample, c, cache_mla_ckv, cache_mla_kpe, cache_na_k, cache_na_v,
              state_dn_fwd, state_dn_bwd, c_ctx, w_ada, b_ada, norm_pre_mix, norm_post_mix,
              norm_pre_mlp, norm_post_mlp, w_in, mla_q_norm, w_q_b, mla_kv_norm, w_kv_b, na_rpb,
              dn_conv, dn_a_log, dn_dt_bias, dn_norm, w_out, w_mlp_in, w_mlp_out):
    y_p = x_prompt
    y_s = x_sample
    ckv_l, kpe_l, nak_l, nav_l, sf_l, sb_l = [], [], [], [], [], []
    for l in range(DEPTH):
        p = dict(w_in=w_in[l], mla_q_norm=mla_q_norm[l], w_q_b=w_q_b[l], mla_kv_norm=mla_kv_norm[l],
                 w_kv_b=w_kv_b[l], na_rpb=na_rpb[l], dn_conv=dn_conv[l], dn_a_log=dn_a_log[l],
                 dn_dt_bias=dn_dt_bias[l], dn_norm=dn_norm[l], w_out=w_out[l])
        sh1, sc1, gt1, sh2, sc2, gt2 = ada_mods(c_ctx, w_ada[l], b_ada[l])
        mo, st = mixer_context(modulate(rmsnorm(y_p, norm_pre_mix[l]), sh1, sc1), p)
        y_p = y_p + gt1 * rmsnorm(mo, norm_post_mix[l])
        h = modulate(rmsnorm(y_p, norm_pre_mlp[l]), sh2, sc2)
        y_p = y_p + gt2 * rmsnorm(mlp(h, w_mlp_in[l], w_mlp_out[l]), norm_post_mlp[l])
        ckv_l.append(st[0]); kpe_l.append(st[1]); nak_l.append(st[2])
        nav_l.append(st[3]); sf_l.append(st[4]); sb_l.append(st[5])
        sh1, sc1, gt1, sh2, sc2, gt2 = ada_mods(c, w_ada[l], b_ada[l])
        mo = mixer_latent(modulate(rmsnorm(y_s, norm_pre_mix[l]), sh1, sc1), p,
                          cache_mla_ckv[:, l], cache_mla_kpe[:, l], cache_na_k[:, l], cache_na_v[:, l],
                          state_dn_fwd[:, l], state_dn_bwd[:, l])
        y_s = y_s + gt1 * rmsnorm(mo, norm_post_mix[l])
        h = modulate(rmsnorm(y_s, norm_pre_mlp[l]), sh2, sc2)
        y_s = y_s + gt2 * rmsnorm(mlp(h, w_mlp_in[l], w_mlp_out[l]), norm_post_mlp[l])
    new_mla_ckv = jnp.stack(ckv_l, 1)
    new_mla_kpe = jnp.stack(kpe_l, 1)
    new_na_k = jnp.stack(nak_l, 1)
    new_na_v = jnp.stack(nav_l, 1)
    new_dn_fwd = jnp.stack(sf_l, 1)
    new_dn_bwd = jnp.stack(sb_l, 1)
    return (y_p, y_s, new_mla_ckv, new_mla_kpe, new_na_k, new_na_v, new_dn_fwd, new_dn_bwd)
```

```python
import functools

import numpy as np
import jax
import jax.numpy as jnp
from jax import lax
from jax.experimental import pallas as pl
from jax.experimental.pallas import tpu as pltpu

BF = jnp.bfloat16
F32 = jnp.float32
EPS = 1e-6
NEG = -1e30

GRID_W = 64
H = 16
Q_LORA = 512
KV_LORA = 256
NOPE = 128
ROPE = 64
HD = 128
QK_PAD = 256
ROPE_THETA = 10000.0
WIN_R = 8
WIN_C = 16
CONV_K = 5
CHUNK = 64
GROUP = 256
MIB = 1024 * 1024


def _cp(sem, vmem_mib):
    return pltpu.CompilerParams(dimension_semantics=sem, vmem_limit_bytes=vmem_mib * MIB)


def _dot(a, b):
    return jnp.dot(a, b, preferred_element_type=F32)


def _dot_nt(a, b):
    return lax.dot_general(a, b, (((1,), (1,)), ((), ())), preferred_element_type=F32)


def _dot_tn(a, b):
    return lax.dot_general(a, b, (((0,), (0,)), ((), ())), preferred_element_type=F32)


def _rms(x, g):
    return x * lax.rsqrt(jnp.mean(x * x, -1, keepdims=True) + EPS) * g


def _silu(x):
    return x * jax.nn.sigmoid(x)


def _ada_kernel(c_ref, w_ref, b_ref, o_ref):
    s = _silu(c_ref[...]).astype(BF)
    o_ref[...] = _dot(s, w_ref[...].astype(BF)) + b_ref[...]


def ada_mods(cvecs, w, b):
    R, D = cvecs.shape
    N = w.shape[1]
    tn = 1024
    return pl.pallas_call(
        _ada_kernel, grid=(N // tn,),
        in_specs=[pl.BlockSpec((R, D), lambda j: (0, 0)),
                  pl.BlockSpec((D, tn), lambda j: (0, j)),
                  pl.BlockSpec((1, tn), lambda j: (0, j))],
        out_specs=pl.BlockSpec((R, tn), lambda j: (0, j)),
        out_shape=jax.ShapeDtypeStruct((R, N), F32),
        compiler_params=_cp(("arbitrary",), 40), name="ada",
    )(cvecs, w, b.reshape(1, N))


def _proj_kernel(x_ref, g_ref, sh_ref, sc_ref, w_ref, cs_ref, o_ref, h_ref):
    @pl.when(pl.program_id(1) == 0)
    def _():
        y = _rms(x_ref[...], g_ref[...])
        h_ref[...] = (y * (1.0 + sc_ref[0]) + sh_ref[0]).astype(BF)

    o_ref[...] = (_dot(h_ref[...], w_ref[...]) * cs_ref[...]).astype(o_ref.dtype)


def proj(x, g, sh, sc, w, cs, out_dtype, tm, tn):
    M, K = x.shape
    N = w.shape[1]
    tiles_per_mod = (M // sh.shape[0]) // tm
    return pl.pallas_call(
        _proj_kernel, grid=(M // tm, N // tn),
        in_specs=[pl.BlockSpec((tm, K), lambda i, j: (i, 0)),
                  pl.BlockSpec((1, K), lambda i, j: (0, 0)),
                  pl.BlockSpec((1, 1, K), lambda i, j: (i // tiles_per_mod, 0, 0)),
                  pl.BlockSpec((1, 1, K), lambda i, j: (i // tiles_per_mod, 0, 0)),
                  pl.BlockSpec((K, tn), lambda i, j: (0, j)),
                  pl.BlockSpec((1, tn), lambda i, j: (0, j))],
        out_specs=pl.BlockSpec((tm, tn), lambda i, j: (i, j)),
        out_shape=jax.ShapeDtypeStruct((M, N), out_dtype),
        scratch_shapes=[pltpu.VMEM((tm, K), BF)],
        compiler_params=_cp(("arbitrary", "arbitrary"), 48), name="proj",
    )(x, g.reshape(1, K), sh, sc, w, cs)


def _mlaq_kernel(x_ref, g_ref, w_ref, ca_ref, cb_ref, o_ref, h_ref, *, scale):
    @pl.when(pl.program_id(1) == 0)
    def _():
        h_ref[...] = _rms(x_ref[...], g_ref[...]).astype(BF)

    y = _dot(h_ref[...], w_ref[...])
    ca = ca_ref[...]
    cb = cb_ref[...]
    for hh in range(y.shape[1] // QK_PAD):
        yh = y[:, hh * QK_PAD:(hh + 1) * QK_PAD]
        part = pltpu.roll(yh, QK_PAD - ROPE, axis=1)
        o_ref[:, hh * QK_PAD:(hh + 1) * QK_PAD] = ((yh * ca + part * cb) * scale).astype(o_ref.dtype)


def mla_q(small, g, w, ca, cb, scale, tm, tn):
    M = small.shape[0]
    N = w.shape[1]
    tab_tiles = ca.shape[0] // tm
    return pl.pallas_call(
        functools.partial(_mlaq_kernel, scale=scale), grid=(M // tm, N // tn),
        in_specs=[pl.BlockSpec((tm, Q_LORA), lambda i, j: (i, 1)),
                  pl.BlockSpec((1, Q_LORA), lambda i, j: (0, 0)),
                  pl.BlockSpec((Q_LORA, tn), lambda i, j: (0, j)),
                  pl.BlockSpec((tm, QK_PAD), lambda i, j: (i % tab_tiles, 0)),
                  pl.BlockSpec((tm, QK_PAD), lambda i, j: (i % tab_tiles, 0))],
        out_specs=pl.BlockSpec((tm, tn), lambda i, j: (i, j)),
        out_shape=jax.ShapeDtypeStruct((M, N), BF),
        scratch_shapes=[pltpu.VMEM((tm, Q_LORA), BF)],
        compiler_params=_cp(("arbitrary", "arbitrary"), 40), name="mla_q",
    )(small, g.reshape(1, Q_LORA), w, ca, cb)


def _kv_kernel(x_ref, g_ref, wk_ref, wv_ref, e1_ref, e2_ref, cos_ref, sin_ref, k_ref, v_ref, c_ref,
               *, tiles_per_batch, norm_tiles):
    x = x_ref[...]
    c = x[:, :KV_LORA]
    cn = _rms(c, g_ref[...])
    if norm_tiles < tiles_per_batch:
        cn = jnp.where(pl.program_id(0) % tiles_per_batch < norm_tiles, cn, c)
    c_ref[...] = cn
    kpe = x[:, KV_LORA:KV_LORA + ROPE]
    pc = (kpe * cos_ref[...]).astype(BF)
    ps = (kpe * sin_ref[...]).astype(BF)
    cb = cn.astype(BF)
    k = _dot(cb, wk_ref[...]) + _dot(pc, e1_ref[...]) + _dot(ps, e2_ref[...])
    k_ref[...] = k.astype(BF)
    v_ref[...] = _dot(cb, wv_ref[...]).astype(BF)


def mla_kv(x, g, wk, wv, e1, e2, cos, sin, tm, norm_tiles):
    M = x.shape[0]
    tiles_per_batch = cos.shape[0] // tm
    NK = wk.shape[1]
    NV = wv.shape[1]
    kern = functools.partial(_kv_kernel, tiles_per_batch=tiles_per_batch, norm_tiles=norm_tiles)
    return pl.pallas_call(
        kern, grid=(M // tm,),
        in_specs=[pl.BlockSpec((tm, 384), lambda i: (i, 0)),
                  pl.BlockSpec((1, KV_LORA), lambda i: (0, 0)),
                  pl.BlockSpec((KV_LORA, NK), lambda i: (0, 0)),
                  pl.BlockSpec((KV_LORA, NV), lambda i: (0, 0)),
                  pl.BlockSpec((ROPE, NK), lambda i: (0, 0)),
                  pl.BlockSpec((ROPE, NK), lambda i: (0, 0)),
                  pl.BlockSpec((tm, ROPE), lambda i: (i % tiles_per_batch, 0)),
                  pl.BlockSpec((tm, ROPE), lambda i: (i % tiles_per_batch, 0))],
        out_specs=[pl.BlockSpec((tm, NK), lambda i: (i, 0)),
                   pl.BlockSpec((tm, NV), lambda i: (i, 0)),
                   pl.BlockSpec((tm, KV_LORA), lambda i: (i, 0))],
        out_shape=[jax.ShapeDtypeStruct((M, NK), BF), jax.ShapeDtypeStruct((M, NV), BF),
                   jax.ShapeDtypeStruct((M, KV_LORA), F32)],
        compiler_params=_cp(("arbitrary",), 40), name="mla_kv",
    )(x, g.reshape(1, KV_LORA), wk, wv, e1, e2, cos, sin)


def _softmax_pv(scores, values):
    m = None
    for s in scores:
        ms = jnp.max(s, -1, keepdims=True)
        m = ms if m is None else jnp.maximum(m, ms)
    l = None
    o = None
    for s, v in zip(scores, values):
        p = jnp.exp(s - m)
        ls = jnp.sum(p, -1, keepdims=True)
        os_ = _dot(p.astype(BF), v)
        l = ls if l is None else l + ls
        o = os_ if o is None else o + os_
    return o / l


def _attn_kernel(*refs, n_seg, hg, dk, dv, scale):
    q_ref = refs[0]
    o_ref = refs[1 + 2 * n_seg]
    for h in range(hg):
        q = q_ref[0, :, h * dk:(h + 1) * dk]
        if scale is not None:
            q = (q.astype(F32) * scale).astype(BF)
        scores, values = [], []
        for sidx in range(n_seg):
            k = refs[1 + 2 * sidx][0, :, h * dk:(h + 1) * dk].astype(BF)
            scores.append(_dot_nt(q, k))
            values.append(refs[2 + 2 * sidx][0, :, h * dv:(h + 1) * dv].astype(BF))
        o_ref[0, :, h * dv:(h + 1) * dv] = _softmax_pv(scores, values).astype(o_ref.dtype)


def attention(q, segs, dk, dv, tq, hg, scale=None):
    B, Tq, _ = q.shape
    in_specs = [pl.BlockSpec((1, tq, hg * dk), lambda b, g, i: (b, i, g))]
    args = [q]
    for k, v in segs:
        Tk = k.shape[1]
        in_specs.append(pl.BlockSpec((1, Tk, hg * dk), lambda b, g, i: (b, 0, g)))
        in_specs.append(pl.BlockSpec((1, Tk, hg * dv), lambda b, g, i: (b, 0, g)))
        args += [k, v]
    kern = functools.partial(_attn_kernel, n_seg=len(segs), hg=hg, dk=dk, dv=dv, scale=scale)
    return pl.pallas_call(
        kern, grid=(B, H // hg, Tq // tq), in_specs=in_specs,
        out_specs=pl.BlockSpec((1, tq, hg * dv), lambda b, g, i: (b, i, g)),
        out_shape=jax.ShapeDtypeStruct((B, Tq, H * dv), BF),
        compiler_params=_cp(("arbitrary", "arbitrary", "arbitrary"), 48), name="attention",
    )(*args)


NA_ROWS = 2
NA_BAND = 10


def _na_tables(rows):
    wr = min(WIN_R, rows)
    nb = min(NA_BAND, rows)
    cs = np.clip(np.arange(GRID_W) - WIN_C // 2, 0, GRID_W - WIN_C)
    pats, pid, out = {}, [], []
    for blk in range(rows // NA_ROWS):
        r0 = blk * NA_ROWS
        bs = int(np.clip(r0 - wr // 2, 0, rows - nb))
        qi = np.arange(NA_ROWS * GRID_W)
        qr, qc = r0 + qi // GRID_W, qi % GRID_W
        kj = np.arange(nb * GRID_W)
        kr, kc = bs + kj // GRID_W, kj % GRID_W
        rs = np.clip(qr - wr // 2, 0, rows - wr)
        valid = ((kr[None] >= rs[:, None]) & (kr[None] < rs[:, None] + wr)
                 & (kc[None] >= cs[qc][:, None]) & (kc[None] < cs[qc][:, None] + WIN_C))
        assert (valid.sum(1) == wr * WIN_C).all()
        dr = np.where(valid, kr[None] - qr[:, None] + WIN_R - 1, 0).astype(np.int32)
        dc = np.where(valid, kc[None] - qc[:, None] + WIN_C - 1, 0).astype(np.int32)
        key = (valid.tobytes(), dr.tobytes(), dc.tobytes())
        if key not in pats:
            pats[key] = len(out)
            out.append((valid, dr, dc))
        pid.append(pats[key])
    valid = np.stack([o[0] for o in out])
    dr = np.stack([o[1] for o in out])
    dc = np.stack([o[2] for o in out])
    return np.asarray(pid, np.int32), valid, dr, dc, nb, wr


def _na_kernel(pid_ref, q_ref, k_ref, v_ref, kc_ref, vc_ref, bias_ref, o_ref, *, hg, nb, wr, rows):
    del pid_ref
    r0 = pl.program_id(2) * NA_ROWS
    bs = jnp.clip(r0 - wr // 2, 0, rows - nb)
    start = pl.multiple_of(bs * GRID_W, GRID_W)
    for h in range(hg):
        q = q_ref[0, :, h * HD:(h + 1) * HD]
        kb = k_ref[0, pl.ds(start, nb * GRID_W), h * HD:(h + 1) * HD]
        vb = v_ref[0, pl.ds(start, nb * GRID_W), h * HD:(h + 1) * HD]
        s_win = _dot_nt(q, kb) + bias_ref[0, h]
        s_ctx = _dot_nt(q, kc_ref[0, :, h * HD:(h + 1) * HD].astype(BF))
        o = _softmax_pv([s_win, s_ctx], [vb, vc_ref[0, :, h * HD:(h + 1) * HD].astype(BF)])
        o_ref[0, :, h * HD:(h + 1) * HD] = o.astype(o_ref.dtype)


def neighbourhood_attention(q, k, v, k_ctx, v_ctx, rpb, hg=4):
    B, T, _ = q.shape
    P = k_ctx.shape[1]
    rows = T // GRID_W
    pid, valid, dr, dc, nb, wr = _na_tables(rows)
    bias = jnp.where(valid[None], rpb[:, dr, dc], NEG).transpose(1, 0, 2, 3)
    rq, nk = NA_ROWS * GRID_W, nb * GRID_W
    kern = functools.partial(_na_kernel, hg=hg, nb=nb, wr=wr, rows=rows)
    grid_spec = pltpu.PrefetchScalarGridSpec(
        num_scalar_prefetch=1, grid=(B, H // hg, rows // NA_ROWS),
        in_specs=[pl.BlockSpec((1, rq, hg * HD), lambda b, g, i, pid: (b, i, g)),
                  pl.BlockSpec((1, T, hg * HD), lambda b, g, i, pid: (b, 0, g)),
                  pl.BlockSpec((1, T, hg * HD), lambda b, g, i, pid: (b, 0, g)),
                  pl.BlockSpec((1, P, hg * HD), lambda b, g, i, pid: (b, 0, g)),
                  pl.BlockSpec((1, P, hg * HD), lambda b, g, i, pid: (b, 0, g)),
                  pl.BlockSpec((1, hg, rq, nk), lambda b, g, i, pid: (pid[i], g, 0, 0))],
        out_specs=pl.BlockSpec((1, rq, hg * HD), lambda b, g, i, pid: (b, i, g)))
    return pl.pallas_call(
        kern, grid_spec=grid_spec, out_shape=jax.ShapeDtypeStruct((B, T, H * HD), BF),
        compiler_params=_cp(("arbitrary", "arbitrary", "arbitrary"), 48), name="na",
    )(jnp.asarray(pid), q, k, v, k_ctx, v_ctx, bias)


def _gate_kernel(x_ref, coef_ref, bias_ref, o_ref):
    x = x_ref[:, 64:128]
    lane = lax.broadcasted_iota(jnp.int32, x.shape, 1)
    row = lax.broadcasted_iota(jnp.int32, x.shape, 0) % CHUNK
    a = x + bias_ref[...]
    g = -jnp.exp(coef_ref[...]) * (jnp.maximum(a, 0.0) + jnp.log1p(jnp.exp(-jnp.abs(a))))
    fwd = lane < 48
    gc = g
    s = 1
    while s < CHUNK:
        up = pltpu.roll(gc, s, axis=0)
        dn = pltpu.roll(gc, gc.shape[0] - s, axis=0)
        gc = gc + jnp.where(fwd, jnp.where(row >= s, up, 0.0), jnp.where(row < CHUNK - s, dn, 0.0))
        s *= 2
    o_ref[...] = jnp.where(lane < 32, jax.nn.sigmoid(x), gc)


def dn_gates(small, a_log, dt_bias, tm):
    M = small.shape[0]
    zero = jnp.zeros((32,), F32)
    coef = jnp.concatenate([zero, a_log.reshape(-1)]).reshape(1, 64)
    bias = jnp.concatenate([zero, dt_bias.reshape(-1)]).reshape(1, 64)
    return pl.pallas_call(
        _gate_kernel, grid=(M // tm,),
        in_specs=[pl.BlockSpec((tm, 128), lambda i: (i, 2)),
                  pl.BlockSpec((1, 64), lambda i: (0, 0)),
                  pl.BlockSpec((1, 64), lambda i: (0, 0))],
        out_specs=pl.BlockSpec((tm, 64), lambda i: (i, 0)),
        out_shape=jax.ShapeDtypeStruct((M, 64), F32),
        compiler_params=_cp(("arbitrary",), 32), name="dn_gates",
    )(small, coef, bias)


def _conv_group(x_ref, w_ref, gi, ng):
    T = x_ref.shape[1]
    base = pl.multiple_of(gi * GROUP, GROUP)
    x = x_ref[0, pl.ds(base, GROUP), :]
    prev = x_ref[0, pl.ds(pl.multiple_of(jnp.maximum(base - 8, 0), 8), 8), :]
    nxt = x_ref[0, pl.ds(pl.multiple_of(jnp.minimum(base + GROUP, T - 8), 8), 8), :]
    prev = jnp.where(gi > 0, prev, 0.0)
    nxt = jnp.where(gi < ng - 1, nxt, 0.0)
    xw = jnp.concatenate([prev, x, nxt], axis=0)
    w = w_ref[...]
    pad = CONV_K // 2
    y = None
    for t in range(CONV_K):
        term = xw[8 - pad + t:8 - pad + t + GROUP, :] * w[t:t + 1, :]
        y = term if y is None else y + term
    return _silu(y)


def _l2n(x):
    return x * lax.rsqrt(jnp.sum(x * x, -1, keepdims=True) + EPS)


def _dn_group(q, k, v, gc_row, b_row, S, backward):
    n = GROUP
    ri = lax.broadcasted_iota(jnp.int32, (n, n), 0)
    ci = lax.broadcasted_iota(jnp.int32, (n, n), 1)
    same = (ri // CHUNK) == (ci // CHUNK)
    eye = ri == ci
    if backward:
        incl = same & (ri <= ci)
        strict = same & (ri < ci)
        last = ci == (ri // CHUNK) * CHUNK
    else:
        incl = same & (ri >= ci)
        strict = same & (ri > ci)
        last = ci == (ri // CHUNK) * CHUNK + (CHUNK - 1)
    gcr = jnp.broadcast_to(gc_row, (n, n))
    gcc = jnp.sum(jnp.where(eye, gcr, 0.0), -1, keepdims=True)
    glc = jnp.sum(jnp.where(last, gcr, 0.0), -1, keepdims=True)
    bc = jnp.sum(jnp.where(eye, jnp.broadcast_to(b_row, (n, n)), 0.0), -1, keepdims=True)
    decay = jnp.exp(jnp.where(incl, gcc - gcr, NEG))
    kb = k * bc
    vb = v * bc
    k16 = k.astype(BF)
    A = jnp.where(strict, _dot_nt(kb.astype(BF), k16) * decay, 0.0)
    lvl = ri ^ ci
    P = jnp.where(eye, 1.0, 0.0) - jnp.where(lvl == 1, A, 0.0)
    for kk in range(1, 6):
        E = jnp.where((lvl >= 2 ** kk) & (lvl < 2 ** (kk + 1)), A, 0.0).astype(BF)
        P16 = P.astype(BF)
        P = P - _dot(_dot(P16, E).astype(BF), P16)
    eg = jnp.exp(gcc)
    rhs = jnp.concatenate([vb, kb * eg], axis=1).astype(BF)
    wk = _dot(P.astype(BF), rhs)
    w_val = wk[:, :HD]
    k_cum = wk[:, HD:].astype(BF)
    attn = jnp.where(incl, _dot_nt(q.astype(BF), k16) * decay, 0.0).astype(BF)
    k_tail = (k * jnp.exp(glc - gcc)).astype(BF)
    q_dec = (q * eg).astype(BF)
    nchunk = n // CHUNK
    order = range(nchunk - 1, -1, -1) if backward else range(nchunk)
    v_new = [None] * nchunk
    o_int = [None] * nchunk
    for c in order:
        sl = slice(c * CHUNK, (c + 1) * CHUNK)
        S16 = S.astype(BF)
        vn = w_val[sl] - _dot(k_cum[sl], S16)
        o_int[c] = _dot(q_dec[sl], S16)
        v_new[c] = vn
        S = S * jnp.exp(glc[c * CHUNK:c * CHUNK + 1, :]) + _dot_tn(k_tail[sl], vn.astype(BF))
    o = jnp.concatenate(o_int, axis=0) + _dot(attn, jnp.concatenate(v_new, axis=0).astype(BF))
    return o, S


def _dn_kernel(q_ref, k_ref, v_ref, z_ref, gate_ref, wq_ref, wk_ref, wv_ref, gn_ref, s0f_ref, s0b_ref,
               o_ref, sf_ref, sb_ref, qn_s, kn_s, vn_s, of_s, ob_s):
    T = q_ref.shape[1]
    ng = T // GROUP

    def prep(gi, carry):
        base = pl.multiple_of(gi * GROUP, GROUP)
        qn_s[pl.ds(base, GROUP), :] = _l2n(_conv_group(q_ref, wq_ref, gi, ng)) * (HD ** -0.5)
        kn_s[pl.ds(base, GROUP), :] = _l2n(_conv_group(k_ref, wk_ref, gi, ng))
        vn_s[pl.ds(base, GROUP), :] = _conv_group(v_ref, wv_ref, gi, ng)
        return carry

    lax.fori_loop(0, ng, prep, 0)

    def load(gi):
        base = pl.multiple_of(gi * GROUP, GROUP)
        return (qn_s[pl.ds(base, GROUP), :], kn_s[pl.ds(base, GROUP), :], vn_s[pl.ds(base, GROUP), :], base)

    def step(i, carry):
        Sf, Sb = carry
        q, k, v, base = load(i)
        o, Sf = _dn_group(q, k, v, gate_ref[0, 2, pl.ds(i, 1), :], gate_ref[0, 0, pl.ds(i, 1), :], Sf, False)
        of_s[pl.ds(base, GROUP), :] = o
        j = ng - 1 - i
        q, k, v, base = load(j)
        o, Sb = _dn_group(q, k, v, gate_ref[0, 3, pl.ds(j, 1), :], gate_ref[0, 1, pl.ds(j, 1), :], Sb, True)
        ob_s[pl.ds(base, GROUP), :] = o
        return Sf, Sb

    Sf, Sb = lax.fori_loop(0, ng, step, (s0f_ref[0, 0, 0], s0b_ref[0, 0, 0]))
    sf_ref[0, 0] = Sf
    sb_ref[0, 0] = Sb

    def fin(gi, carry):
        base = pl.multiple_of(gi * GROUP, GROUP)
        o = of_s[pl.ds(base, GROUP), :] + ob_s[pl.ds(base, GROUP), :]
        z = z_ref[0, pl.ds(base, GROUP), :].astype(F32)
        o_ref[0, pl.ds(base, GROUP), :] = (_rms(o, gn_ref[...]) * _silu(z)).astype(o_ref.dtype)
        return carry

    lax.fori_loop(0, ng, fin, 0)


def deltanet(qkv, zg, gates, conv_w, gn, s0f, s0b, layer, shared):
    B, T, _ = qkv.shape
    tok = lambda off: pl.BlockSpec((1, T, HD), lambda b, h: (b, 0, off + h))
    cw = lambda off: pl.BlockSpec((CONV_K, HD), lambda b, h: (0, off + h))
    s0 = pl.BlockSpec((1, 1, 1, HD, HD),
                      (lambda b, h: (0, 0, 0, 0, 0)) if shared else (lambda b, h: (b, layer, h, 0, 0)))
    st = pl.BlockSpec((1, 1, HD, HD), lambda b, h: (b, h, 0, 0))
    return pl.pallas_call(
        _dn_kernel, grid=(B, H),
        in_specs=[tok(0), tok(H), tok(2 * H), tok(0),
                  pl.BlockSpec((1, 4, T // GROUP, GROUP), lambda b, h: (b * H + h, 0, 0, 0)),
                  cw(0), cw(H), cw(2 * H),
                  pl.BlockSpec((1, HD), lambda b, h: (0, 0)), s0, s0],
        out_specs=[tok(0), st, st],
        out_shape=[jax.ShapeDtypeStruct((B, T, H * HD), BF),
                   jax.ShapeDtypeStruct((B, H, HD, HD), F32), jax.ShapeDtypeStruct((B, H, HD, HD), F32)],
        scratch_shapes=[pltpu.VMEM((T, HD), F32)] * 5,
        compiler_params=_cp(("arbitrary", "arbitrary"), 48), name="deltanet",
    )(qkv, qkv, qkv, zg, gates, conv_w, conv_w, conv_w, gn.reshape(1, HD), s0f, s0b)


def _merge_kernel(x_ref, ga_ref, gb_ref, gc_ref, oa_ref, ob_ref, oc_ref, w_ref, gp_ref, gt_ref, y_ref):
    def gated(g_ref, o_ref):
        return jax.nn.sigmoid(g_ref[...].astype(F32)) * o_ref[...].astype(F32)

    m = gated(ga_ref, oa_ref) + gated(gb_ref, ob_ref) + gated(gc_ref, oc_ref)
    mo = _dot(m.astype(BF), w_ref[...])
    y_ref[...] = x_ref[...] + gt_ref[0] * _rms(mo, gp_ref[...])


def merge_out(x, zg, oa, ob, oc, w, gpost, gt, tm):
    M, D = x.shape
    tiles_per_mod = (M // gt.shape[0]) // tm
    row = lambda c: pl.BlockSpec((tm, D), lambda i: (i, c))
    return pl.pallas_call(
        _merge_kernel, grid=(M // tm,),
        in_specs=[row(0), row(1), row(2), row(3), row(0), row(0), row(0),
                  pl.BlockSpec((D, D), lambda i: (0, 0)),
                  pl.BlockSpec((1, D), lambda i: (0, 0)),
                  pl.BlockSpec((1, 1, D), lambda i: (i // tiles_per_mod, 0, 0))],
        out_specs=row(0), out_shape=jax.ShapeDtypeStruct((M, D), F32),
        compiler_params=_cp(("arbitrary",), 48), name="merge_out",
    )(x, zg, zg, zg, oa, ob, oc, w, gpost.reshape(1, D), gt)


def _mlp_kernel(y_ref, g_ref, sh_ref, sc_ref, w1_ref, w2_ref, gp_ref, gt_ref, o_ref, h_ref, acc_ref):
    f = pl.program_id(1)

    @pl.when(f == 0)
    def _():
        h_ref[...] = (_rms(y_ref[...], g_ref[...]) * (1.0 + sc_ref[0]) + sh_ref[0]).astype(BF)
        acc_ref[...] = jnp.zeros_like(acc_ref)

    u = jnp.maximum(_dot(h_ref[...], w1_ref[...]), 0.0)
    acc_ref[...] += _dot((u * u).astype(BF), w2_ref[...])

    @pl.when(f == pl.num_programs(1) - 1)
    def _():
        o_ref[...] = y_ref[...] + gt_ref[0] * _rms(acc_ref[...], gp_ref[...])


def mlp_block(y, g, sh, sc, w1, w2, gpost, gt, tm, tf):
    M, D = y.shape
    F = w1.shape[1]
    tiles_per_mod = (M // sh.shape[0]) // tm
    mod = pl.BlockSpec((1, 1, D), lambda i, f: (i // tiles_per_mod, 0, 0))
    vec = pl.BlockSpec((1, D), lambda i, f: (0, 0))
    return pl.pallas_call(
        _mlp_kernel, grid=(M // tm, F // tf),
        in_specs=[pl.BlockSpec((tm, D), lambda i, f: (i, 0)), vec, mod, mod,
                  pl.BlockSpec((D, tf), lambda i, f: (0, f)),
                  pl.BlockSpec((tf, D), lambda i, f: (f, 0)), vec, mod],
        out_specs=pl.BlockSpec((tm, D), lambda i, f: (i, 0)),
        out_shape=jax.ShapeDtypeStruct((M, D), F32),
        scratch_shapes=[pltpu.VMEM((tm, D), BF), pltpu.VMEM((tm, D), F32)],
        compiler_params=_cp(("arbitrary", "arbitrary"), 48), name="mlp",
    )(y, g.reshape(1, D), sh, sc, w1, w2, gpost.reshape(1, D), gt)


def _prep_layer(w_in, w_q_b, w_kv_b, D):
    o = np.cumsum([0, Q_LORA, KV_LORA + ROPE, 3 * H * HD, 3 * H * HD, H * HD, 4 * H, 3 * D])
    qa, kva, na, dnqkv, dnz, dnba, gates = [w_in[:, o[i]:o[i + 1]] for i in range(7)]
    w_small = jnp.concatenate([kva, dnba, jnp.zeros((D, 128), F32), qa], 1).astype(BF)
    w_zg = jnp.concatenate([dnz, gates], 1).astype(BF)
    wq = w_q_b.reshape(Q_LORA, H, NOPE + ROPE)
    r = wq[..., NOPE:]
    rh = jnp.concatenate([-r[..., ROPE // 2:], r[..., :ROPE // 2]], -1)
    wq = jnp.concatenate([wq, rh], -1).reshape(Q_LORA, H * QK_PAD).astype(BF)
    wkv = w_kv_b.reshape(KV_LORA, H, NOPE + HD)
    wk = jnp.concatenate([wkv[..., :NOPE], jnp.zeros((KV_LORA, H, QK_PAD - NOPE), F32)], -1)
    wk = wk.reshape(KV_LORA, H * QK_PAD).astype(BF)
    wv = wkv[..., NOPE:].reshape(KV_LORA, H * HD).astype(BF)
    return w_small, na.astype(BF), dnqkv.astype(BF), w_zg, wq, wk, wv


def _rope_embed():
    e1 = np.zeros((ROPE, H, QK_PAD), np.float32)
    e2 = np.zeros((ROPE, H, QK_PAD), np.float32)
    half = ROPE // 2
    for i in range(ROPE):
        e1[i, :, NOPE + i] = 1.0
        if i < half:
            e2[i + half, :, NOPE + i] = -1.0
        else:
            e2[i - half, :, NOPE + i] = 1.0
    return (jnp.asarray(e1.reshape(ROPE, H * QK_PAD), BF), jnp.asarray(e2.reshape(ROPE, H * QK_PAD), BF))


def _rope_tables(T, n_plain):
    n_freq = ROPE // 4
    inv = 1.0 / (ROPE_THETA ** (jnp.arange(n_freq, dtype=F32) / n_freq))
    t = jnp.arange(T)
    row = (t // GRID_W).astype(F32)
    col = (t % GRID_W).astype(F32)
    ang = jnp.concatenate([row[:, None] * inv, col[:, None] * inv], -1)
    cos = jnp.concatenate([jnp.cos(ang), jnp.ones((n_plain, ROPE // 2), F32)], 0)
    sin = jnp.concatenate([jnp.sin(ang), jnp.zeros((n_plain, ROPE // 2), F32)], 0)
    cos = jnp.concatenate([cos, cos], -1)
    sin = jnp.concatenate([sin, sin], -1)
    n = cos.shape[0]
    ca = jnp.concatenate([jnp.ones((n, NOPE), F32), cos, jnp.zeros((n, QK_PAD - NOPE - ROPE), F32)], -1)
    cb = jnp.concatenate([jnp.zeros((n, NOPE), F32), sin, jnp.zeros((n, QK_PAD - NOPE - ROPE), F32)], -1)
    return cos, sin, ca, cb


def _plain_tables(T):
    cos = jnp.ones((T, ROPE), F32)
    sin = jnp.zeros((T, ROPE), F32)
    ca = jnp.concatenate([jnp.ones((T, NOPE + ROPE), F32), jnp.zeros((T, QK_PAD - NOPE - ROPE), F32)], -1)
    return cos, sin, ca, jnp.zeros((T, QK_PAD), F32)


def _gate_rows(gates, B, T):
    g = gates.reshape(B, T // GROUP, GROUP, 4, H)
    return g.transpose(0, 4, 3, 1, 2).reshape(B * H, 4, T // GROUP, GROUP)


def _mixer_front(x, B, T, mods, lw, p, tm):
    D = x.shape[1]
    sh1, sc1 = mods[0], mods[1]
    w_small, w_na, w_dn, w_zg, wq, wk, wv = lw
    ones = lambda n: jnp.ones((1, n), F32)
    small = proj(x, p["norm_pre_mix"], sh1, sc1, w_small, ones(1024), F32, tm, 512)
    zg = proj(x, p["norm_pre_mix"], sh1, sc1, w_zg, ones(4 * D), BF, tm, 1024)
    dn = proj(x, p["norm_pre_mix"], sh1, sc1, w_dn, ones(3 * H * HD), F32, tm, 1024)
    gates = _gate_rows(dn_gates(small, p["dn_a_log"], p["dn_dt_bias"], min(512, B * T)), B, T)
    return small, zg, dn, gates


def layer_context(x, B, T, mods, lw, p, tm):
    D = x.shape[1]
    M = B * T
    sh1, sc1, gt1, sh2, sc2, gt2 = mods
    w_small, w_na, w_dn, w_zg, wq, wk, wv = lw
    small, zg, dn, gates = _mixer_front(x, B, T, mods, lw, p, tm)
    na = proj(x, p["norm_pre_mix"], sh1, sc1, w_na, jnp.ones((1, 3 * H * HD), F32), F32, tm, 1024)
    cos, sin, ca, cb = _plain_tables(T)
    q = mla_q(small, p["mla_q_norm"], wq, ca, cb, (NOPE + ROPE) ** -0.5, 256, 1024)
    e1, e2 = _rope_embed()
    kfull, v, ckv = mla_kv(small, p["mla_kv_norm"], wk, wv, e1, e2, cos, sin, 256, T // 256)
    o_a = attention(q.reshape(B, T, -1), [(kfull.reshape(B, T, -1), v.reshape(B, T, -1))],
                    QK_PAD, HD, T, 4)
    na3 = na.reshape(B, T, 3 * H * HD)
    nq, nk, nv = na3[..., :H * HD], na3[..., H * HD:2 * H * HD], na3[..., 2 * H * HD:]
    o_b = attention(nq, [(nk, nv)], HD, HD, T, 4, scale=HD ** -0.5)
    zero = jnp.zeros((1, 1, 1, HD, HD), F32)
    o_c, s_f, s_b = deltanet(dn.reshape(B, T, -1), zg.reshape(B, T, -1), gates, p["dn_conv"],
                             p["dn_norm"], zero, zero, 0, True)
    y = merge_out(x, zg, o_a.reshape(M, D), o_b.reshape(M, D), o_c.reshape(M, D), p["w_out"],
                  p["norm_post_mix"], gt1, 256)
    y = mlp_block(y, p["norm_pre_mlp"], sh2, sc2, p["w1"], p["w2"], p["norm_post_mlp"], gt2, tm, 512)
    state = (ckv.reshape(B, T, KV_LORA), small[:, KV_LORA:KV_LORA + ROPE].reshape(B, T, ROPE),
             nk.reshape(B, T, H, HD), nv.reshape(B, T, H, HD), s_f, s_b)
    return y, state


def layer_latent(x, B, T, mods, lw, p, tm, caches, layer):
    D = x.shape[1]
    M = B * T
    sh1, sc1, gt1, sh2, sc2, gt2 = mods
    w_small, w_na, w_dn, w_zg, wq, wk, wv = lw
    ckv_c, kpe_c, nak_c, nav_c, s0f, s0b = caches
    P = ckv_c.shape[2]
    small, zg, dn, gates = _mixer_front(x, B, T, mods, lw, p, tm)
    cs = jnp.concatenate([jnp.full((1, H * HD), HD ** -0.5, F32), jnp.ones((1, 2 * H * HD), F32)], 1)
    na = proj(x, p["norm_pre_mix"], sh1, sc1, w_na, cs, BF, tm, 1024)
    cos, sin, ca, cb = _rope_tables(T, P)
    q = mla_q(small, p["mla_q_norm"], wq, ca[:T], cb[:T], (NOPE + ROPE) ** -0.5, 256, 1024)
    ctx_in = jnp.concatenate([ckv_c[:, layer], kpe_c[:, layer], jnp.zeros((B, P, 64), F32)], -1)
    kv_in = jnp.concatenate([small.reshape(B, T, -1)[..., :384], ctx_in], 1).reshape(B * (T + P), 384)
    e1, e2 = _rope_embed()
    kfull, v, _ = mla_kv(kv_in, p["mla_kv_norm"], wk, wv, e1, e2, cos, sin, 256, T // 256)
    kfull = kfull.reshape(B, T + P, -1)
    v = v.reshape(B, T + P, -1)
    o_a = attention(q.reshape(B, T, -1), [(kfull[:, :T], v[:, :T]), (kfull[:, T:], v[:, T:])],
                    QK_PAD, HD, 256, 2)
    na3 = na.reshape(B, T, 3 * H * HD)
    nq, nk, nv = na3[..., :H * HD], na3[..., H * HD:2 * H * HD], na3[..., 2 * H * HD:]
    o_b = neighbourhood_attention(nq, nk, nv, nak_c[:, layer].reshape(B, P, H * HD).astype(BF),
                                  nav_c[:, layer].reshape(B, P, H * HD).astype(BF), p["na_rpb"])
    o_c, _, _ = deltanet(dn.reshape(B, T, -1), zg.reshape(B, T, -1), gates, p["dn_conv"],
                         p["dn_norm"], s0f, s0b, layer, False)
    y = merge_out(x, zg, o_a.reshape(M, D), o_b.reshape(M, D), o_c.reshape(M, D), p["w_out"],
                  p["norm_post_mix"], gt1, 256)
    return mlp_block(y, p["norm_pre_mlp"], sh2, sc2, p["w1"], p["w2"], p["norm_post_mlp"], gt2, tm, 512)


def kernel(x_prompt, x_sample, c, cache_mla_ckv, cache_mla_kpe, cache_na_k, cache_na_v, state_dn_fwd, state_dn_bwd, c_ctx, w_ada, b_ada, norm_pre_mix, norm_post_mix, norm_pre_mlp, norm_post_mlp, w_in, mla_q_norm, w_q_b, mla_kv_norm, w_kv_b, na_rpb, dn_conv, dn_a_log, dn_dt_bias, dn_norm, w_out, w_mlp_in, w_mlp_out):
    Bp, Tp, D = x_prompt.shape
    Bs, Ts, _ = x_sample.shape
    L = w_in.shape[0]
    y_p = x_prompt.reshape(Bp * Tp, D)
    y_s = x_sample.reshape(Bs * Ts, D)
    rows = -(-(1 + Bs) // 8) * 8
    cvecs = jnp.concatenate([c_ctx[None], c, jnp.zeros((rows - 1 - Bs, D), F32)], 0)
    caches = (cache_mla_ckv, cache_mla_kpe, cache_na_k, cache_na_v, state_dn_fwd, state_dn_bwd)
    states = []
    for l in range(L):
        mod = ada_mods(cvecs, w_ada[l], b_ada[l]).reshape(rows, 6, 1, D)
        mods_p = [mod[0:1, k] for k in range(6)]
        mods_s = [mod[1:1 + Bs, k] for k in range(6)]
        lw = _prep_layer(w_in[l], w_q_b[l], w_kv_b[l], D)
        p = dict(norm_pre_mix=norm_pre_mix[l], norm_post_mix=norm_post_mix[l], norm_pre_mlp=norm_pre_mlp[l],
                 norm_post_mlp=norm_post_mlp[l], mla_q_norm=mla_q_norm[l], mla_kv_norm=mla_kv_norm[l],
                 na_rpb=na_rpb[l], dn_conv=dn_conv[l], dn_a_log=dn_a_log[l], dn_dt_bias=dn_dt_bias[l],
                 dn_norm=dn_norm[l], w_out=w_out[l].astype(BF), w1=w_mlp_in[l].astype(BF),
                 w2=w_mlp_out[l].astype(BF))
        y_p, st = layer_context(y_p, Bp, Tp, mods_p, lw, p, 512)
        states.append(st)
        y_s = layer_latent(y_s, Bs, Ts, mods_s, lw, p, 512, caches, l)
    outs = [jnp.stack([s[k] for s in states], 1) for k in range(6)]
    return (y_p.reshape(Bp, Tp, D), y_s.reshape(Bs, Ts, D), *outs)
```

```python
import functools

import numpy as np
import jax
import jax.numpy as jnp
from jax import lax
from jax.experimental import pallas as pl
from jax.experimental.pallas import tpu as pltpu

BF = jnp.bfloat16
F32 = jnp.float32
EPS = 1e-6
NEG = -1e30

GRID_W = 64
H = 16
Q_LORA = 512
KV_LORA = 256
NOPE = 128
ROPE = 64
HD = 128
QK_PAD = 256
ROPE_THETA = 10000.0
WIN_R = 8
WIN_C = 16
CONV_K = 5
CHUNK = 64
GROUP = 256
MIB = 1024 * 1024


def _cp(sem, vmem_mib):
    return pltpu.CompilerParams(dimension_semantics=sem, vmem_limit_bytes=vmem_mib * MIB)


def _dot(a, b):
    return jnp.dot(a, b, preferred_element_type=F32)


def _dot_nt(a, b):
    return lax.dot_general(a, b, (((1,), (1,)), ((), ())), preferred_element_type=F32)


def _dot_tn(a, b):
    return lax.dot_general(a, b, (((0,), (0,)), ((), ())), preferred_element_type=F32)


def _rms(x, g):
    return x * lax.rsqrt(jnp.mean(x * x, -1, keepdims=True) + EPS) * g


def _silu(x):
    return x * jax.nn.sigmoid(x)


def _ada_kernel(c_ref, w_ref, b_ref, o_ref):
    s = _silu(c_ref[...]).astype(BF)
    o_ref[...] = _dot(s, w_ref[...].astype(BF)) + b_ref[...]


def ada_mods(cvecs, w, b):
    R, D = cvecs.shape
    N = w.shape[1]
    tn = 1024
    return pl.pallas_call(
        _ada_kernel, grid=(N // tn,),
        in_specs=[pl.BlockSpec((R, D), lambda j: (0, 0)),
                  pl.BlockSpec((D, tn), lambda j: (0, j)),
                  pl.BlockSpec((1, tn), lambda j: (0, j))],
        out_specs=pl.BlockSpec((R, tn), lambda j: (0, j)),
        out_shape=jax.ShapeDtypeStruct((R, N), F32),
        compiler_params=_cp(("arbitrary",), 40), name="ada",
    )(cvecs, w, b.reshape(1, N))


def _proj_kernel(x_ref, g_ref, sh_ref, sc_ref, w_ref, cs_ref, o_ref, h_ref):
    @pl.when(pl.program_id(1) == 0)
    def _():
        y = _rms(x_ref[...], g_ref[...])
        h_ref[...] = (y * (1.0 + sc_ref[0]) + sh_ref[0]).astype(BF)

    o_ref[...] = (_dot(h_ref[...], w_ref[...]) * cs_ref[...]).astype(o_ref.dtype)


def proj(x, g, sh, sc, w, cs, out_dtype, tm, tn):
    M, K = x.shape
    N = w.shape[1]
    tiles_per_mod = (M // sh.shape[0]) // tm
    return pl.pallas_call(
        _proj_kernel, grid=(M // tm, N // tn),
        in_specs=[pl.BlockSpec((tm, K), lambda i, j: (i, 0)),
                  pl.BlockSpec((1, K), lambda i, j: (0, 0)),
                  pl.BlockSpec((1, 1, K), lambda i, j: (i // tiles_per_mod, 0, 0)),
                  pl.BlockSpec((1, 1, K), lambda i, j: (i // tiles_per_mod, 0, 0)),
                  pl.BlockSpec((K, tn), lambda i, j: (0, j)),
                  pl.BlockSpec((1, tn), lambda i, j: (0, j))],
        out_specs=pl.BlockSpec((tm, tn), lambda i, j: (i, j)),
        out_shape=jax.ShapeDtypeStruct((M, N), out_dtype),
        scratch_shapes=[pltpu.VMEM((tm, K), BF)],
        compiler_params=_cp(("arbitrary", "arbitrary"), 48), name="proj",
    )(x, g.reshape(1, K), sh, sc, w, cs)


def _mlaq_kernel(x_ref, g_ref, w_ref, ca_ref, cb_ref, o_ref, h_ref, *, scale):
    @pl.when(pl.program_id(1) == 0)
    def _():
        h_ref[...] = _rms(x_ref[...], g_ref[...]).astype(BF)

    y = _dot(h_ref[...], w_ref[...])
    ca = ca_ref[...]
    cb = cb_ref[...]
    for hh in range(y.shape[1] // QK_PAD):
        yh = y[:, hh * QK_PAD:(hh + 1) * QK_PAD]
        part = pltpu.roll(yh, QK_PAD - ROPE, axis=1)
        o_ref[:, hh * QK_PAD:(hh + 1) * QK_PAD] = ((yh * ca + part * cb) * scale).astype(o_ref.dtype)


def mla_q(small, g, w, ca, cb, scale, tm, tn):
    M = small.shape[0]
    N = w.shape[1]
    tab_tiles = ca.shape[0] // tm
    return pl.pallas_call(
        functools.partial(_mlaq_kernel, scale=scale), grid=(M // tm, N // tn),
        in_specs=[pl.BlockSpec((tm, Q_LORA), lambda i, j: (i, 1)),
                  pl.BlockSpec((1, Q_LORA), lambda i, j: (0, 0)),
                  pl.BlockSpec((Q_LORA, tn), lambda i, j: (0, j)),
                  pl.BlockSpec((tm, QK_PAD), lambda i, j: (i % tab_tiles, 0)),
                  pl.BlockSpec((tm, QK_PAD), lambda i, j: (i % tab_tiles, 0))],
        out_specs=pl.BlockSpec((tm, tn), lambda i, j: (i, j)),
        out_shape=jax.ShapeDtypeStruct((M, N), BF),
        scratch_shapes=[pltpu.VMEM((tm, Q_LORA), BF)],
        compiler_params=_cp(("arbitrary", "arbitrary"), 40), name="mla_q",
    )(small, g.reshape(1, Q_LORA), w, ca, cb)


def _kv_kernel(x_ref, g_ref, wk_ref, wv_ref, e1_ref, e2_ref, cos_ref, sin_ref, k_ref, v_ref, c_ref,
               *, tiles_per_batch, norm_tiles):
    x = x_ref[...]
    c = x[:, :KV_LORA]
    cn = _rms(c, g_ref[...])
    if norm_tiles < tiles_per_batch:
        cn = jnp.where(pl.program_id(0) % tiles_per_batch < norm_tiles, cn, c)
    c_ref[...] = cn
    kpe = x[:, KV_LORA:KV_LORA + ROPE]
    pc = (kpe * cos_ref[...]).astype(BF)
    ps = (kpe * sin_ref[...]).astype(BF)
    cb = cn.astype(BF)
    k = _dot(cb, wk_ref[...]) + _dot(pc, e1_ref[...]) + _dot(ps, e2_ref[...])
    k_ref[...] = k.astype(BF)
    v_ref[...] = _dot(cb, wv_ref[...]).astype(BF)


def mla_kv(x, g, wk, wv, e1, e2, cos, sin, tm, norm_tiles):
    M = x.shape[0]
    tiles_per_batch = cos.shape[0] // tm
    NK = wk.shape[1]
    NV = wv.shape[1]
    kern = functools.partial(_kv_kernel, tiles_per_batch=tiles_per_batch, norm_tiles=norm_tiles)
    return pl.pallas_call(
        kern, grid=(M // tm,),
        in_specs=[pl.BlockSpec((tm, 384), lambda i: (i, 0)),
                  pl.BlockSpec((1, KV_LORA), lambda i: (0, 0)),
                  pl.BlockSpec((KV_LORA, NK), lambda i: (0, 0)),
                  pl.BlockSpec((KV_LORA, NV), lambda i: (0, 0)),
                  pl.BlockSpec((ROPE, NK), lambda i: (0, 0)),
                  pl.BlockSpec((ROPE, NK), lambda i: (0, 0)),
                  pl.BlockSpec((tm, ROPE), lambda i: (i % tiles_per_batch, 0)),
                  pl.BlockSpec((tm, ROPE), lambda i: (i % tiles_per_batch, 0))],
        out_specs=[pl.BlockSpec((tm, NK), lambda i: (i, 0)),
                   pl.BlockSpec((tm, NV), lambda i: (i, 0)),
                   pl.BlockSpec((tm, KV_LORA), lambda i: (i, 0))],
        out_shape=[jax.ShapeDtypeStruct((M, NK), BF), jax.ShapeDtypeStruct((M, NV), BF),
                   jax.ShapeDtypeStruct((M, KV_LORA), F32)],
        compiler_params=_cp(("arbitrary",), 40), name="mla_kv",
    )(x, g.reshape(1, KV_LORA), wk, wv, e1, e2, cos, sin)


def _softmax_pv(scores, values):
    m = None
    for s in scores:
        ms = jnp.max(s, -1, keepdims=True)
        m = ms if m is None else jnp.maximum(m, ms)
    l = None
    o = None
    for s, v in zip(scores, values):
        p = jnp.exp(s - m)
        ls = jnp.sum(p, -1, keepdims=True)
        os_ = _dot(p.astype(BF), v)
        l = ls if l is None else l + ls
        o = os_ if o is None else o + os_
    return o / l


def _attn_kernel(*refs, n_seg, hg, dk, dv, scale):
    q_ref = refs[0]
    o_ref = refs[1 + 2 * n_seg]
    for h in range(hg):
        q = q_ref[0, :, h * dk:(h + 1) * dk]
        if scale is not None:
            q = (q.astype(F32) * scale).astype(BF)
        scores, values = [], []
        for sidx in range(n_seg):
            k = refs[1 + 2 * sidx][0, :, h * dk:(h + 1) * dk].astype(BF)
            scores.append(_dot_nt(q, k))
            values.append(refs[2 + 2 * sidx][0, :, h * dv:(h + 1) * dv].astype(BF))
        o_ref[0, :, h * dv:(h + 1) * dv] = _softmax_pv(scores, values).astype(o_ref.dtype)


def attention(q, segs, dk, dv, tq, hg, scale=None):
    B, Tq, _ = q.shape
    in_specs = [pl.BlockSpec((1, tq, hg * dk), lambda b, g, i: (b, i, g))]
    args = [q]
    for k, v in segs:
        Tk = k.shape[1]
        in_specs.append(pl.BlockSpec((1, Tk, hg * dk), lambda b, g, i: (b, 0, g)))
        in_specs.append(pl.BlockSpec((1, Tk, hg * dv), lambda b, g, i: (b, 0, g)))
        args += [k, v]
    kern = functools.partial(_attn_kernel, n_seg=len(segs), hg=hg, dk=dk, dv=dv, scale=scale)
    return pl.pallas_call(
        kern, grid=(B, H // hg, Tq // tq), in_specs=in_specs,
        out_specs=pl.BlockSpec((1, tq, hg * dv), lambda b, g, i: (b, i, g)),
        out_shape=jax.ShapeDtypeStruct((B, Tq, H * dv), BF),
        compiler_params=_cp(("arbitrary", "arbitrary", "arbitrary"), 48), name="attention",
    )(*args)


NA_ROWS = 2
NA_BAND = 10


def _na_tables(rows):
    wr = min(WIN_R, rows)
    nb = min(NA_BAND, rows)
    cs = np.clip(np.arange(GRID_W) - WIN_C // 2, 0, GRID_W - WIN_C)
    pats, pid, out = {}, [], []
    for blk in range(rows // NA_ROWS):
        r0 = blk * NA_ROWS
        bs = int(np.clip(r0 - wr // 2, 0, rows - nb))
        qi = np.arange(NA_ROWS * GRID_W)
        qr, qc = r0 + qi // GRID_W, qi % GRID_W
        kj = np.arange(nb * GRID_W)
        kr, kc = bs + kj // GRID_W, kj % GRID_W
        rs = np.clip(qr - wr // 2, 0, rows - wr)
        valid = ((kr[None] >= rs[:, None]) & (kr[None] < rs[:, None] + wr)
                 & (kc[None] >= cs[qc][:, None]) & (kc[None] < cs[qc][:, None] + WIN_C))
        assert (valid.sum(1) == wr * WIN_C).all()
        dr = np.clip(bs + np.arange(nb)[None] - (r0 + np.arange(NA_ROWS))[:, None] + WIN_R - 1, 0, 2 * WIN_R - 2)
        key = (valid.tobytes(), dr.tobytes())
        if key not in pats:
            pats[key] = len(out)
            out.append((valid, dr))
        pid.append(pats[key])
    valid = np.stack([o[0] for o in out])
    dr = np.stack([o[1] for o in out])
    return np.asarray(pid, np.int32), valid, dr, nb, wr


def _na_bias(rpb, valid, dr):
    span = 2 * WIN_C - 1
    rp = jnp.pad(rpb, ((0, 0), (0, 0), (GRID_W - WIN_C, GRID_W - WIN_C)))
    off = GRID_W - 1
    tz = jnp.stack([rp[:, :, off - c:off - c + GRID_W] for c in range(GRID_W)], 2)
    assert rp.shape[-1] == span + 2 * (GRID_W - WIN_C) and tz.shape[-1] == GRID_W
    tiles = [jnp.concatenate([jnp.concatenate([tz[:, d] for d in dr_p[i]], -1) for i in range(dr.shape[1])], 1)
             for dr_p in dr.tolist()]
    return jnp.where(valid[:, None], jnp.stack(tiles, 0), NEG)


def _na_kernel(pid_ref, q_ref, k_ref, v_ref, kc_ref, vc_ref, bias_ref, o_ref, *, hg, nb, wr, rows):
    del pid_ref
    r0 = pl.program_id(2) * NA_ROWS
    bs = jnp.clip(r0 - wr // 2, 0, rows - nb)
    start = pl.multiple_of(bs * GRID_W, GRID_W)
    for h in range(hg):
        q = q_ref[0, :, h * HD:(h + 1) * HD]
        kb = k_ref[0, pl.ds(start, nb * GRID_W), h * HD:(h + 1) * HD]
        vb = v_ref[0, pl.ds(start, nb * GRID_W), h * HD:(h + 1) * HD]
        s_win = _dot_nt(q, kb) + bias_ref[0, h]
        s_ctx = _dot_nt(q, kc_ref[0, :, h * HD:(h + 1) * HD].astype(BF))
        o = _softmax_pv([s_win, s_ctx], [vb, vc_ref[0, :, h * HD:(h + 1) * HD].astype(BF)])
        o_ref[0, :, h * HD:(h + 1) * HD] = o.astype(o_ref.dtype)


def neighbourhood_attention(q, k, v, k_ctx, v_ctx, rpb, hg=4):
    B, T, _ = q.shape
    P = k_ctx.shape[1]
    rows = T // GRID_W
    pid, valid, dr, nb, wr = _na_tables(rows)
    bias = _na_bias(rpb, valid, dr)
    rq, nk = NA_ROWS * GRID_W, nb * GRID_W
    kern = functools.partial(_na_kernel, hg=hg, nb=nb, wr=wr, rows=rows)
    grid_spec = pltpu.PrefetchScalarGridSpec(
        num_scalar_prefetch=1, grid=(B, H // hg, rows // NA_ROWS),
        in_specs=[pl.BlockSpec((1, rq, hg * HD), lambda b, g, i, pid: (b, i, g)),
                  pl.BlockSpec((1, T, hg * HD), lambda b, g, i, pid: (b, 0, g)),
                  pl.BlockSpec((1, T, hg * HD), lambda b, g, i, pid: (b, 0, g)),
                  pl.BlockSpec((1, P, hg * HD), lambda b, g, i, pid: (b, 0, g)),
                  pl.BlockSpec((1, P, hg * HD), lambda b, g, i, pid: (b, 0, g)),
                  pl.BlockSpec((1, hg, rq, nk), lambda b, g, i, pid: (pid[i], g, 0, 0))],
        out_specs=pl.BlockSpec((1, rq, hg * HD), lambda b, g, i, pid: (b, i, g)))
    return pl.pallas_call(
        kern, grid_spec=grid_spec, out_shape=jax.ShapeDtypeStruct((B, T, H * HD), BF),
        compiler_params=_cp(("arbitrary", "arbitrary", "arbitrary"), 48), name="na",
    )(jnp.asarray(pid), q, k, v, k_ctx, v_ctx, bias)


def _gate_kernel(x_ref, coef_ref, bias_ref, o_ref):
    x = x_ref[:, 64:128]
    lane = lax.broadcasted_iota(jnp.int32, x.shape, 1)
    row = lax.broadcasted_iota(jnp.int32, x.shape, 0) % CHUNK
    a = x + bias_ref[...]
    g = -jnp.exp(coef_ref[...]) * (jnp.maximum(a, 0.0) + jnp.log1p(jnp.exp(-jnp.abs(a))))
    fwd = lane < 48
    gc = g
    s = 1
    while s < CHUNK:
        up = pltpu.roll(gc, s, axis=0)
        dn = pltpu.roll(gc, gc.shape[0] - s, axis=0)
        gc = gc + jnp.where(fwd, jnp.where(row >= s, up, 0.0), jnp.where(row < CHUNK - s, dn, 0.0))
        s *= 2
    o_ref[...] = jnp.where(lane < 32, jax.nn.sigmoid(x), gc)


def dn_gates(small, a_log, dt_bias, tm):
    M = small.shape[0]
    zero = jnp.zeros((32,), F32)
    coef = jnp.concatenate([zero, a_log.reshape(-1)]).reshape(1, 64)
    bias = jnp.concatenate([zero, dt_bias.reshape(-1)]).reshape(1, 64)
    return pl.pallas_call(
        _gate_kernel, grid=(M // tm,),
        in_specs=[pl.BlockSpec((tm, 128), lambda i: (i, 2)),
                  pl.BlockSpec((1, 64), lambda i: (0, 0)),
                  pl.BlockSpec((1, 64), lambda i: (0, 0))],
        out_specs=pl.BlockSpec((tm, 64), lambda i: (i, 0)),
        out_shape=jax.ShapeDtypeStruct((M, 64), F32),
        compiler_params=_cp(("arbitrary",), 32), name="dn_gates",
    )(small, coef, bias)


def _dn_prep_kernel(x_ref, w_ref, o_ref):
    T = x_ref.shape[1]
    ng = T // GROUP
    col = pl.program_id(1)
    w = w_ref[...]
    pad = CONV_K // 2

    def body(gi, carry):
        base = pl.multiple_of(gi * GROUP, GROUP)
        x = x_ref[0, pl.ds(base, GROUP), :]
        prev = x_ref[0, pl.ds(pl.multiple_of(jnp.maximum(base - 8, 0), 8), 8), :]
        nxt = x_ref[0, pl.ds(pl.multiple_of(jnp.minimum(base + GROUP, T - 8), 8), 8), :]
        xw = jnp.concatenate([jnp.where(gi > 0, prev, 0.0), x, jnp.where(gi < ng - 1, nxt, 0.0)], axis=0)
        y = None
        for t in range(CONV_K):
            term = xw[8 - pad + t:8 - pad + t + GROUP, :] * w[t:t + 1, :]
            y = term if y is None else y + term
        y = _silu(y)
        n = y * lax.rsqrt(jnp.sum(y * y, -1, keepdims=True) + EPS) * jnp.where(col < H, HD ** -0.5, 1.0)
        o_ref[0, pl.ds(base, GROUP), :] = jnp.where(col < 2 * H, n, y)
        return carry

    lax.fori_loop(0, ng, body, 0)


def dn_prep(qkv, conv_w):
    B, T, C = qkv.shape
    blk = pl.BlockSpec((1, T, HD), lambda b, j: (b, 0, j))
    return pl.pallas_call(
        _dn_prep_kernel, grid=(B, C // HD),
        in_specs=[blk, pl.BlockSpec((CONV_K, HD), lambda b, j: (0, j))],
        out_specs=blk, out_shape=jax.ShapeDtypeStruct((B, T, C), F32),
        compiler_params=_cp(("arbitrary", "arbitrary"), 32), name="dn_prep",
    )(qkv, conv_w)


def _dn_masks(backward):
    n = GROUP
    ri = lax.broadcasted_iota(jnp.int32, (n, n), 0)
    ci = lax.broadcasted_iota(jnp.int32, (n, n), 1)
    same = (ri // CHUNK) == (ci // CHUNK)
    eye = ri == ci
    if backward:
        incl = same & (ri <= ci)
        strict = same & (ri < ci)
        last = ci == (ri // CHUNK) * CHUNK
    else:
        incl = same & (ri >= ci)
        strict = same & (ri > ci)
        last = ci == (ri // CHUNK) * CHUNK + (CHUNK - 1)
    return eye, incl, strict, last, ri ^ ci


def _dn_group(q, k, v, gc_row, b_row, S, masks, backward):
    n = GROUP
    eye, incl, strict, last, lvl = masks
    gcr = jnp.broadcast_to(gc_row, (n, n))
    gcc = jnp.sum(jnp.where(eye, gcr, 0.0), -1, keepdims=True)
    glc = jnp.sum(jnp.where(last, gcr, 0.0), -1, keepdims=True)
    bc = jnp.sum(jnp.where(eye, jnp.broadcast_to(b_row, (n, n)), 0.0), -1, keepdims=True)
    decay = jnp.exp(jnp.where(incl, gcc - gcr, NEG))
    kb = k * bc
    vb = v * bc
    k16 = k.astype(BF)
    A = jnp.where(strict, _dot_nt(kb.astype(BF), k16) * decay, 0.0)
    P = jnp.where(eye, 1.0, 0.0) - jnp.where(lvl == 1, A, 0.0)
    for kk in range(1, 6):
        E = jnp.where((lvl >= 2 ** kk) & (lvl < 2 ** (kk + 1)), A, 0.0).astype(BF)
        P16 = P.astype(BF)
        P = P - _dot(_dot(P16, E).astype(BF), P16)
    eg = jnp.exp(gcc)
    rhs = jnp.concatenate([vb, kb * eg], axis=1).astype(BF)
    wk = _dot(P.astype(BF), rhs)
    w_val = wk[:, :HD]
    k_cum = wk[:, HD:].astype(BF)
    attn = jnp.where(incl, _dot_nt(q.astype(BF), k16) * decay, 0.0).astype(BF)
    k_tail = (k * jnp.exp(glc - gcc)).astype(BF)
    q_dec = (q * eg).astype(BF)
    nchunk = n // CHUNK
    order = range(nchunk - 1, -1, -1) if backward else range(nchunk)
    v_new = [None] * nchunk
    o_int = [None] * nchunk
    for c in order:
        sl = slice(c * CHUNK, (c + 1) * CHUNK)
        S16 = S.astype(BF)
        vn = w_val[sl] - _dot(k_cum[sl], S16)
        o_int[c] = _dot(q_dec[sl], S16)
        v_new[c] = vn
        S = S * jnp.exp(glc[c * CHUNK:c * CHUNK + 1, :]) + _dot_tn(k_tail[sl], vn.astype(BF))
    o = jnp.concatenate(o_int, axis=0) + _dot(attn, jnp.concatenate(v_new, axis=0).astype(BF))
    return o, S


def _dn_scan_kernel(*refs, hg, backward):
    if backward:
        q_ref, k_ref, v_ref, gate_ref, s0_ref, of_ref, z_ref, gn_ref, o_ref, sfin_ref, S_s = refs
    else:
        q_ref, k_ref, v_ref, gate_ref, s0_ref, o_ref, sfin_ref, S_s = refs
    i = pl.program_id(2)
    g = pl.program_id(1)

    @pl.when(i == 0)
    def _():
        S_s[...] = s0_ref[0, 0]

    masks = _dn_masks(backward)
    kind = 1 if backward else 0
    for h in range(hg):
        sl = slice(h * HD, (h + 1) * HD)
        row = pl.ds(g * hg + h, 1)
        o, S = _dn_group(q_ref[0, :, sl], k_ref[0, :, sl], v_ref[0, :, sl], gate_ref[0, 0, 2 + kind, row, :],
                         gate_ref[0, 0, kind, row, :], S_s[h], masks, backward)
        S_s[h] = S
        if backward:
            o = o + of_ref[0, :, sl]
            o_ref[0, :, sl] = (_rms(o, gn_ref[...]) * _silu(z_ref[0, :, sl].astype(F32))).astype(o_ref.dtype)
        else:
            o_ref[0, :, sl] = o

    @pl.when(i == pl.num_programs(2) - 1)
    def _():
        sfin_ref[0] = S_s[...]


def dn_scan(qkv, gates, s0, layer, shared, backward, hg, o_fwd=None, zg=None, gn=None):
    B, T, _ = qkv.shape
    ng = T // GROUP
    G = H // hg
    gidx = (lambda i: ng - 1 - i) if backward else (lambda i: i)
    tok = lambda off: pl.BlockSpec((1, GROUP, hg * HD), lambda b, g, i: (b, gidx(i), off + g))
    s0_spec = pl.BlockSpec((1, 1, hg, HD, HD), (lambda b, g, i: (0, 0, 0, 0, 0)) if shared
                           else (lambda b, g, i: (b, layer, g, 0, 0)))
    in_specs = [tok(0), tok(G), tok(2 * G),
                pl.BlockSpec((1, 1, 4, H, GROUP), lambda b, g, i: (b, gidx(i), 0, 0, 0)), s0_spec]
    args = [qkv, qkv, qkv, gates, s0]
    if backward:
        in_specs += [tok(0), tok(0), pl.BlockSpec((1, HD), lambda b, g, i: (0, 0))]
        args += [o_fwd, zg, gn.reshape(1, HD)]
    return pl.pallas_call(
        functools.partial(_dn_scan_kernel, hg=hg, backward=backward), grid=(B, G, ng), in_specs=in_specs,
        out_specs=[tok(0), pl.BlockSpec((1, hg, HD, HD), lambda b, g, i: (b, g, 0, 0))],
        out_shape=[jax.ShapeDtypeStruct((B, T, H * HD), BF if backward else F32),
                   jax.ShapeDtypeStruct((B, H, HD, HD), F32)],
        scratch_shapes=[pltpu.VMEM((hg, HD, HD), F32)],
        compiler_params=_cp(("arbitrary", "arbitrary", "arbitrary"), 48),
        name="dn_scan_bwd" if backward else "dn_scan_fwd",
    )(*args)


def deltanet(qkv, zg, gates, conv_w, gn, s0f, s0b, layer, shared, hg=4):
    qkv = dn_prep(qkv, conv_w)
    o_f, s_f = dn_scan(qkv, gates, s0f, layer, shared, False, hg)
    o, s_b = dn_scan(qkv, gates, s0b, layer, shared, True, hg, o_f, zg, gn)
    return o, s_f, s_b


def _merge_kernel(x_ref, ga_ref, gb_ref, gc_ref, oa_ref, ob_ref, oc_ref, w_ref, gp_ref, gt_ref, y_ref):
    def gated(g_ref, o_ref):
        return jax.nn.sigmoid(g_ref[...].astype(F32)) * o_ref[...].astype(F32)

    m = gated(ga_ref, oa_ref) + gated(gb_ref, ob_ref) + gated(gc_ref, oc_ref)
    mo = _dot(m.astype(BF), w_ref[...])
    y_ref[...] = x_ref[...] + gt_ref[0] * _rms(mo, gp_ref[...])


def merge_out(x, zg, oa, ob, oc, w, gpost, gt, tm):
    M, D = x.shape
    tiles_per_mod = (M // gt.shape[0]) // tm
    row = lambda c: pl.BlockSpec((tm, D), lambda i: (i, c))
    return pl.pallas_call(
        _merge_kernel, grid=(M // tm,),
        in_specs=[row(0), row(1), row(2), row(3), row(0), row(0), row(0),
                  pl.BlockSpec((D, D), lambda i: (0, 0)),
                  pl.BlockSpec((1, D), lambda i: (0, 0)),
                  pl.BlockSpec((1, 1, D), lambda i: (i // tiles_per_mod, 0, 0))],
        out_specs=row(0), out_shape=jax.ShapeDtypeStruct((M, D), F32),
        compiler_params=_cp(("arbitrary",), 48), name="merge_out",
    )(x, zg, zg, zg, oa, ob, oc, w, gpost.reshape(1, D), gt)


def _mlp_kernel(y_ref, g_ref, sh_ref, sc_ref, w1_ref, w2_ref, gp_ref, gt_ref, o_ref, h_ref, acc_ref):
    f = pl.program_id(1)

    @pl.when(f == 0)
    def _():
        h_ref[...] = (_rms(y_ref[...], g_ref[...]) * (1.0 + sc_ref[0]) + sh_ref[0]).astype(BF)
        acc_ref[...] = jnp.zeros_like(acc_ref)

    u = jnp.maximum(_dot(h_ref[...], w1_ref[...]), 0.0)
    acc_ref[...] += _dot((u * u).astype(BF), w2_ref[...])

    @pl.when(f == pl.num_programs(1) - 1)
    def _():
        o_ref[...] = y_ref[...] + gt_ref[0] * _rms(acc_ref[...], gp_ref[...])


def mlp_block(y, g, sh, sc, w1, w2, gpost, gt, tm, tf):
    M, D = y.shape
    F = w1.shape[1]
    tiles_per_mod = (M // sh.shape[0]) // tm
    mod = pl.BlockSpec((1, 1, D), lambda i, f: (i // tiles_per_mod, 0, 0))
    vec = pl.BlockSpec((1, D), lambda i, f: (0, 0))
    return pl.pallas_call(
        _mlp_kernel, grid=(M // tm, F // tf),
        in_specs=[pl.BlockSpec((tm, D), lambda i, f: (i, 0)), vec, mod, mod,
                  pl.BlockSpec((D, tf), lambda i, f: (0, f)),
                  pl.BlockSpec((tf, D), lambda i, f: (f, 0)), vec, mod],
        out_specs=pl.BlockSpec((tm, D), lambda i, f: (i, 0)),
        out_shape=jax.ShapeDtypeStruct((M, D), F32),
        scratch_shapes=[pltpu.VMEM((tm, D), BF), pltpu.VMEM((tm, D), F32)],
        compiler_params=_cp(("arbitrary", "arbitrary"), 48), name="mlp",
    )(y, g.reshape(1, D), sh, sc, w1, w2, gpost.reshape(1, D), gt)


def _prep_layer(w_in, w_q_b, w_kv_b, D):
    o = np.cumsum([0, Q_LORA, KV_LORA + ROPE, 3 * H * HD, 3 * H * HD, H * HD, 4 * H, 3 * D])
    qa, kva, na, dnqkv, dnz, dnba, gates = [w_in[:, o[i]:o[i + 1]] for i in range(7)]
    w_small = jnp.concatenate([kva, dnba, jnp.zeros((D, 128), F32), qa], 1).astype(BF)
    w_zg = jnp.concatenate([dnz, gates], 1).astype(BF)
    wq = w_q_b.reshape(Q_LORA, H, NOPE + ROPE)
    r = wq[..., NOPE:]
    rh = jnp.concatenate([-r[..., ROPE // 2:], r[..., :ROPE // 2]], -1)
    wq = jnp.concatenate([wq, rh], -1).reshape(Q_LORA, H * QK_PAD).astype(BF)
    wkv = w_kv_b.reshape(KV_LORA, H, NOPE + HD)
    wk = jnp.concatenate([wkv[..., :NOPE], jnp.zeros((KV_LORA, H, QK_PAD - NOPE), F32)], -1)
    wk = wk.reshape(KV_LORA, H * QK_PAD).astype(BF)
    wv = wkv[..., NOPE:].reshape(KV_LORA, H * HD).astype(BF)
    return w_small, na.astype(BF), dnqkv.astype(BF), w_zg, wq, wk, wv


def _rope_embed():
    e1 = np.zeros((ROPE, H, QK_PAD), np.float32)
    e2 = np.zeros((ROPE, H, QK_PAD), np.float32)
    half = ROPE // 2
    for i in range(ROPE):
        e1[i, :, NOPE + i] = 1.0
        if i < half:
            e2[i + half, :, NOPE + i] = -1.0
        else:
            e2[i - half, :, NOPE + i] = 1.0
    return (jnp.asarray(e1.reshape(ROPE, H * QK_PAD), BF), jnp.asarray(e2.reshape(ROPE, H * QK_PAD), BF))


def _rope_tables(T, n_plain):
    n_freq = ROPE // 4
    inv = 1.0 / (ROPE_THETA ** (jnp.arange(n_freq, dtype=F32) / n_freq))
    t = jnp.arange(T)
    row = (t // GRID_W).astype(F32)
    col = (t % GRID_W).astype(F32)
    ang = jnp.concatenate([row[:, None] * inv, col[:, None] * inv], -1)
    cos = jnp.concatenate([jnp.cos(ang), jnp.ones((n_plain, ROPE // 2), F32)], 0)
    sin = jnp.concatenate([jnp.sin(ang), jnp.zeros((n_plain, ROPE // 2), F32)], 0)
    cos = jnp.concatenate([cos, cos], -1)
    sin = jnp.concatenate([sin, sin], -1)
    n = cos.shape[0]
    ca = jnp.concatenate([jnp.ones((n, NOPE), F32), cos, jnp.zeros((n, QK_PAD - NOPE - ROPE), F32)], -1)
    cb = jnp.concatenate([jnp.zeros((n, NOPE), F32), sin, jnp.zeros((n, QK_PAD - NOPE - ROPE), F32)], -1)
    return cos, sin, ca, cb


def _plain_tables(T):
    cos = jnp.ones((T, ROPE), F32)
    sin = jnp.zeros((T, ROPE), F32)
    ca = jnp.concatenate([jnp.ones((T, NOPE + ROPE), F32), jnp.zeros((T, QK_PAD - NOPE - ROPE), F32)], -1)
    return cos, sin, ca, jnp.zeros((T, QK_PAD), F32)


def _gate_rows(gates, B, T):
    return gates.reshape(B, T // GROUP, GROUP, 4, H).transpose(0, 1, 3, 4, 2)


def _mixer_front(x, B, T, mods, lw, p, tm):
    D = x.shape[1]
    sh1, sc1 = mods[0], mods[1]
    w_small, w_na, w_dn, w_zg, wq, wk, wv = lw
    ones = lambda n: jnp.ones((1, n), F32)
    small = proj(x, p["norm_pre_mix"], sh1, sc1, w_small, ones(1024), F32, tm, 512)
    zg = proj(x, p["norm_pre_mix"], sh1, sc1, w_zg, ones(4 * D), BF, tm, 1024)
    dn = proj(x, p["norm_pre_mix"], sh1, sc1, w_dn, ones(3 * H * HD), F32, tm, 1024)
    gates = _gate_rows(dn_gates(small, p["dn_a_log"], p["dn_dt_bias"], min(512, B * T)), B, T)
    return small, zg, dn, gates


def layer_context(x, B, T, mods, lw, p, tm):
    D = x.shape[1]
    M = B * T
    sh1, sc1, gt1, sh2, sc2, gt2 = mods
    w_small, w_na, w_dn, w_zg, wq, wk, wv = lw
    small, zg, dn, gates = _mixer_front(x, B, T, mods, lw, p, tm)
    na = proj(x, p["norm_pre_mix"], sh1, sc1, w_na, jnp.ones((1, 3 * H * HD), F32), F32, tm, 1024)
    cos, sin, ca, cb = _plain_tables(T)
    q = mla_q(small, p["mla_q_norm"], wq, ca, cb, (NOPE + ROPE) ** -0.5, 256, 1024)
    e1, e2 = _rope_embed()
    kfull, v, ckv = mla_kv(small, p["mla_kv_norm"], wk, wv, e1, e2, cos, sin, 256, T // 256)
    o_a = attention(q.reshape(B, T, -1), [(kfull.reshape(B, T, -1), v.reshape(B, T, -1))],
                    QK_PAD, HD, T, 4)
    na3 = na.reshape(B, T, 3 * H * HD)
    nq, nk, nv = na3[..., :H * HD], na3[..., H * HD:2 * H * HD], na3[..., 2 * H * HD:]
    o_b = attention(nq, [(nk, nv)], HD, HD, T, 4, scale=HD ** -0.5)
    zero = jnp.zeros((1, 1, 4, HD, HD), F32)
    o_c, s_f, s_b = deltanet(dn.reshape(B, T, -1), zg.reshape(B, T, -1), gates, p["dn_conv"],
                             p["dn_norm"], zero, zero, 0, True)
    y = merge_out(x, zg, o_a.reshape(M, D), o_b.reshape(M, D), o_c.reshape(M, D), p["w_out"],
                  p["norm_post_mix"], gt1, 256)
    y = mlp_block(y, p["norm_pre_mlp"], sh2, sc2, p["w1"], p["w2"], p["norm_post_mlp"], gt2, tm, 512)
    state = (ckv.reshape(B, T, KV_LORA), small[:, KV_LORA:KV_LORA + ROPE].reshape(B, T, ROPE),
             nk.reshape(B, T, H, HD), nv.reshape(B, T, H, HD), s_f, s_b)
    return y, state


def layer_latent(x, B, T, mods, lw, p, tm, caches, layer):
    D = x.shape[1]
    M = B * T
    sh1, sc1, gt1, sh2, sc2, gt2 = mods
    w_small, w_na, w_dn, w_zg, wq, wk, wv = lw
    ckv_c, kpe_c, nak_c, nav_c, s0f, s0b = caches
    P = ckv_c.shape[2]
    small, zg, dn, gates = _mixer_front(x, B, T, mods, lw, p, tm)
    cs = jnp.concatenate([jnp.full((1, H * HD), HD ** -0.5, F32), jnp.ones((1, 2 * H * HD), F32)], 1)
    na = proj(x, p["norm_pre_mix"], sh1, sc1, w_na, cs, BF, tm, 1024)
    cos, sin, ca, cb = _rope_tables(T, P)
    q = mla_q(small, p["mla_q_norm"], wq, ca[:T], cb[:T], (NOPE + ROPE) ** -0.5, 256, 1024)
    ctx_in = jnp.concatenate([ckv_c[:, layer], kpe_c[:, layer], jnp.zeros((B, P, 64), F32)], -1)
    kv_in = jnp.concatenate([small.reshape(B, T, -1)[..., :384], ctx_in], 1).reshape(B * (T + P), 384)
    e1, e2 = _rope_embed()
    kfull, v, _ = mla_kv(kv_in, p["mla_kv_norm"], wk, wv, e1, e2, cos, sin, 256, T // 256)
    kfull = kfull.reshape(B, T + P, -1)
    v = v.reshape(B, T + P, -1)
    o_a = attention(q.reshape(B, T, -1), [(kfull[:, :T], v[:, :T]), (kfull[:, T:], v[:, T:])],
                    QK_PAD, HD, 256, 2)
    na3 = na.reshape(B, T, 3 * H * HD)
    nq, nk, nv = na3[..., :H * HD], na3[..., H * HD:2 * H * HD], na3[..., 2 * H * HD:]
    o_b = neighbourhood_attention(nq, nk, nv, nak_c[:, layer].reshape(B, P, H * HD).astype(BF),
                                  nav_c[:, layer].reshape(B, P, H * HD).astype(BF), p["na_rpb"])
    o_c, _, _ = deltanet(dn.reshape(B, T, -1), zg.reshape(B, T, -1), gates, p["dn_conv"],
                         p["dn_norm"], s0f, s0b, layer, False)
    y = merge_out(x, zg, o_a.reshape(M, D), o_b.reshape(M, D), o_c.reshape(M, D), p["w_out"],
                  p["norm_post_mix"], gt1, 256)
    return mlp_block(y, p["norm_pre_mlp"], sh2, sc2, p["w1"], p["w2"], p["norm_post_mlp"], gt2, tm, 512)


def kernel(x_prompt, x_sample, c, cache_mla_ckv, cache_mla_kpe, cache_na_k, cache_na_v, state_dn_fwd, state_dn_bwd, c_ctx, w_ada, b_ada, norm_pre_mix, norm_post_mix, norm_pre_mlp, norm_post_mlp, w_in, mla_q_norm, w_q_b, mla_kv_norm, w_kv_b, na_rpb, dn_conv, dn_a_log, dn_dt_bias, dn_norm, w_out, w_mlp_in, w_mlp_out):
    Bp, Tp, D = x_prompt.shape
    Bs, Ts, _ = x_sample.shape
    L = w_in.shape[0]
    y_p = x_prompt.reshape(Bp * Tp, D)
    y_s = x_sample.reshape(Bs * Ts, D)
    rows = -(-(1 + Bs) // 8) * 8
    cvecs = jnp.concatenate([c_ctx[None], c, jnp.zeros((rows - 1 - Bs, D), F32)], 0)
    caches = (cache_mla_ckv, cache_mla_kpe, cache_na_k, cache_na_v, state_dn_fwd, state_dn_bwd)
    states = []
    for l in range(L):
        mod = ada_mods(cvecs, w_ada[l], b_ada[l]).reshape(rows, 6, 1, D)
        mods_p = [mod[0:1, k] for k in range(6)]
        mods_s = [mod[1:1 + Bs, k] for k in range(6)]
        lw = _prep_layer(w_in[l], w_q_b[l], w_kv_b[l], D)
        p = dict(norm_pre_mix=norm_pre_mix[l], norm_post_mix=norm_post_mix[l], norm_pre_mlp=norm_pre_mlp[l],
                 norm_post_mlp=norm_post_mlp[l], mla_q_norm=mla_q_norm[l], mla_kv_norm=mla_kv_norm[l],
                 na_rpb=na_rpb[l], dn_conv=dn_conv[l], dn_a_log=dn_a_log[l], dn_dt_bias=dn_dt_bias[l],
                 dn_norm=dn_norm[l], w_out=w_out[l].astype(BF), w1=w_mlp_in[l].astype(BF),
                 w2=w_mlp_out[l].astype(BF))
        y_p, st = layer_context(y_p, Bp, Tp, mods_p, lw, p, 512)
        states.append(st)
        y_s = layer_latent(y_s, Bs, Ts, mods_s, lw, p, 512, caches, l)
    outs = [jnp.stack([s[k] for s in states], 1) for k in range(6)]
    return (y_p.reshape(Bp, Tp, D), y_s.reshape(Bs, Ts, D), *outs)
```
